```python
import jax
import jax.numpy as jnp
from jax import lax
import numpy as np

D_MODEL = 1024
BATCH = 16
SEQ = 4096
DEPTH = 2

CHUNK = 64
Q_BLOCK = 128
HEAD_DIM = 128
MIX_WIDTH = D_MODEL // 2
N_HEADS = MIX_WIDTH // HEAD_DIM
N_BRANCHES = 3
N_GROUPS = 4
EXPERTS_PER_GROUP = 8
N_EXPERTS = N_GROUPS * EXPERTS_PER_GROUP
TOP_K = 2
D_FF_EXPERT = D_MODEL // 2
MOE_BLOCK = 128
DEEPNORM_ALPHA = (2 * DEPTH) ** 0.25
DEEPNORM_BETA = (8 * DEPTH) ** -0.25
LN_EPS = 1e-5
HEAD_NORM_EPS = 1e-6
RET_ROPE_BASE = 10000.0
IN_SIZES = (MIX_WIDTH,) * 4 + (MIX_WIDTH,) * 3 + (N_HEADS,) + (MIX_WIDTH,) * 4 + (D_MODEL,) * N_BRANCHES
D_IN = sum(IN_SIZES)

kernel_name = "hybrid_hgrn2_fox_retnet_hmoe_deepnorm"


def layer_norm(x, g, b):
    xf = x.astype(jnp.float32)
    mu = xf.mean(-1, keepdims=True)
    var = jnp.square(xf - mu).mean(-1, keepdims=True)
    return ((xf - mu) * lax.rsqrt(var + LN_EPS) * g + b).astype(x.dtype)


def head_rms_norm(t):
    return t * lax.rsqrt(jnp.mean(jnp.square(t), -1, keepdims=True) + HEAD_NORM_EPS)


def head_layer_norm(t):
    mu = t.mean(-1, keepdims=True)
    c = t - mu
    return c * lax.rsqrt(jnp.mean(jnp.square(c), -1, keepdims=True) + HEAD_NORM_EPS)


def to_chunks(t):
    b, s, h, d = t.shape
    return t.reshape(b, s // CHUNK, CHUNK, h, d).transpose(1, 0, 3, 2, 4)


def from_chunks(t):
    n, b, h, c, d = t.shape
    return t.transpose(1, 0, 3, 2, 4).reshape(b, n * c, h, d)


def rope(t, pos):
    half = t.shape[-1] // 2
    inv = 1.0 / (RET_ROPE_BASE ** jnp.linspace(0.0, 1.0, half, dtype=jnp.float32))
    ang = pos[:, None] * inv[None, :]
    cos = jnp.cos(ang)[None, :, None, :]
    sin = jnp.sin(ang)[None, :, None, :]
    t1, t2 = t[..., :half], t[..., half:]
    return jnp.concatenate([t1 * cos - t2 * sin, t1 * sin + t2 * cos], axis=-1)


def hgrn2_chunk_scan(q, k, v, log_f):
    bsz, _, h, dk = q.shape
    dv = v.shape[-1]
    causal = jnp.tril(jnp.ones((CHUNK, CHUNK), dtype=bool))[:, :, None]

    def step(state, xs):
        qc, kc, vc, lfc = xs
        cum = jnp.cumsum(lfc, axis=2)
        diff = cum[:, :, :, None, :] - cum[:, :, None, :, :]
        decay = jnp.where(causal, jnp.exp(jnp.where(causal, diff, 0.0)), 0.0)
        scores = jnp.einsum('bhtd,bhsd,bhtsd->bhts', qc, kc, decay)
        o = (jnp.einsum('bhts,bhsv->bhtv', scores, vc)
             + jnp.einsum('bhtd,bhdv->bhtv', qc * jnp.exp(cum), state))
        last = cum[:, :, -1:, :]
        state = (jnp.exp(last[:, :, 0, :])[..., None] * state
                 + jnp.einsum('bhsd,bhsv->bhdv', kc * jnp.exp(last - cum), vc))
        return state, o

    s0 = jnp.zeros((bsz, h, dk, dv), jnp.float32)
    _, o = lax.scan(step, s0, (to_chunks(q), to_chunks(k), to_chunks(v), to_chunks(log_f)))
    return from_chunks(o)


def retention_chunk_scan(q, k, v, log_gamma):
    bsz, _, h, dk = q.shape
    dv = v.shape[-1]
    idx = jnp.arange(CHUNK, dtype=jnp.float32)
    rel = idx[:, None] - idx[None, :]
    decay_mask = jnp.where(rel >= 0, jnp.exp(log_gamma[:, None, None] * jnp.maximum(rel, 0.0)), 0.0)
    q_decay = jnp.exp(log_gamma[:, None] * (idx + 1.0))[..., None]
    k_decay = jnp.exp(log_gamma[:, None] * (CHUNK - 1.0 - idx))[..., None]
    chunk_decay = jnp.exp(log_gamma * CHUNK)[:, None, None]

    def step(state, xs):
        qc, kc, vc = xs
        inner = jnp.einsum('bhtd,bhsd->bhts', qc, kc) * decay_mask
        o = (jnp.einsum('bhts,bhsv->bhtv', inner, vc)
             + jnp.einsum('bhtd,bhdv->bhtv', qc * q_decay, state))
        state = chunk_decay * state + jnp.einsum('bhsd,bhsv->bhdv', kc * k_decay, vc)
        return state, o

    s0 = jnp.zeros((bsz, h, dk, dv), jnp.float32)
    _, o = lax.scan(step, s0, (to_chunks(q), to_chunks(k), to_chunks(v)))
    return from_chunks(o)


def forgetting_attention(q, k, v, log_f):
    bsz, seq, h, dh = q.shape
    n_blocks = seq // Q_BLOCK
    cum = jnp.cumsum(log_f, axis=1).transpose(0, 2, 1)
    kh = k.transpose(0, 2, 1, 3)
    vh = v.transpose(0, 2, 1, 3)
    q_blocks = q.reshape(bsz, n_blocks, Q_BLOCK, h, dh).transpose(1, 0, 3, 2, 4)
    cum_blocks = cum.reshape(bsz, h, n_blocks, Q_BLOCK).transpose(2, 0, 1, 3)
    key_pos = jnp.arange(seq)
    scale = dh ** -0.5

    def block(args):
        qb, cum_q, blk = args
        logits = (jnp.einsum('bhqd,bhkd->bhqk', qb, kh) * scale
                  + cum_q[..., None] - cum[:, :, None, :])
        q_pos = blk * Q_BLOCK + jnp.arange(Q_BLOCK)
        logits = jnp.where(key_pos[None, :] <= q_pos[:, None], logits, -jnp.inf)
        return jnp.einsum('bhqk,bhkd->bhqd', jax.nn.softmax(logits, axis=-1), vh)

    o = lax.map(block, (q_blocks, cum_blocks, jnp.arange(n_blocks)))
    return o.transpose(1, 0, 3, 2, 4).reshape(bsz, seq, h, dh)


def token_mixer(x, w_in, w_branch, w_out, fox_bias, lower_bound):
    bsz, seq, _ = x.shape
    f32 = jnp.float32
    split_at = np.cumsum(IN_SIZES)[:-1].tolist()
    (a_q, a_f, a_i, a_g, b_q, b_k, b_v, b_f, c_q, c_k, c_v, c_g,
     g_a, g_b, g_c) = jnp.split(x @ w_in, split_at, axis=-1)

    def heads(t):
        return t.reshape(bsz, seq, N_HEADS, -1).astype(f32)

    log_fa = jnp.logaddexp(jnp.log(lower_bound), jnp.log1p(-lower_bound) + jax.nn.log_sigmoid(a_f.astype(f32)))
    key_a = -jnp.expm1(log_fa)
    o_a = hgrn2_chunk_scan(heads(a_q), heads(key_a), heads(a_i), heads(log_fa))
    y_a = head_rms_norm(o_a) * jax.nn.sigmoid(heads(a_g))

    log_fb = jax.nn.log_sigmoid(b_f.astype(f32) + fox_bias.astype(f32))
    y_b = forgetting_attention(heads(b_q), heads(b_k), heads(b_v), log_fb)

    pos = jnp.arange(seq, dtype=f32)
    log_gamma = jnp.log(1.0 - jnp.power(2.0, -5.0 - jnp.arange(N_HEADS, dtype=f32)))
    o_c = retention_chunk_scan(rope(heads(c_q), pos), rope(heads(c_k), pos) * HEAD_DIM ** -0.5,
                               heads(c_v), log_gamma)
    y_c = head_layer_norm(o_c) * jax.nn.silu(heads(c_g))

    def flat(t):
        return t.reshape(bsz, seq, MIX_WIDTH).astype(x.dtype)

    merged = (jax.nn.sigmoid(g_a) * (flat(y_a) @ w_branch[0])
              + jax.nn.sigmoid(g_b) * (flat(y_b) @ w_branch[1])
              + jax.nn.sigmoid(g_c) * (flat(y_c) @ w_branch[2]))
    return merged @ w_out


def hierarchical_moe(x, w_rg, w_re, w_up, w_gate, w_down):
    bsz, seq, d = x.shape
    n_tok = bsz * seq
    xf = x.reshape(n_tok, d)
    group_logits = (xf @ w_rg).astype(jnp.float32)
    group_idx = jnp.argmax(group_logits, axis=-1)
    group_prob = jnp.take_along_axis(jax.nn.softmax(group_logits, axis=-1), group_idx[:, None], axis=-1)
    expert_logits = (xf @ w_re).astype(jnp.float32).reshape(n_tok, N_GROUPS, EXPERTS_PER_GROUP)
    in_group = jnp.take_along_axis(expert_logits, group_idx[:, None, None], axis=1)[:, 0]
    top_val, top_idx = lax.top_k(in_group, TOP_K)
    gate = (group_prob * jax.nn.softmax(top_val, axis=-1)).reshape(-1).astype(x.dtype)
    expert_id = (group_idx[:, None] * EXPERTS_PER_GROUP + top_idx).reshape(-1)
    token_id = jnp.repeat(jnp.arange(n_tok, dtype=jnp.int32), TOP_K)

    n_assign = n_tok * TOP_K
    order = jnp.argsort(expert_id)
    sorted_e = expert_id[order]
    counts = jnp.bincount(expert_id, length=N_EXPERTS)
    starts = jnp.cumsum(counts) - counts
    padded = (counts + MOE_BLOCK - 1) // MOE_BLOCK * MOE_BLOCK
    padded_end = jnp.cumsum(padded)
    padded_start = padded_end - padded
    dest = padded_start[sorted_e] + jnp.arange(n_assign) - starts[sorted_e]
    n_rows = n_assign + N_EXPERTS * MOE_BLOCK
    n_blocks = n_rows // MOE_BLOCK
    row_token = jnp.zeros((n_rows,), jnp.int32).at[dest].set(token_id[order])
    row_gate = jnp.zeros((n_rows,), x.dtype).at[dest].set(gate[order])
    block_start = jnp.arange(n_blocks) * MOE_BLOCK
    block_expert = jnp.minimum(jnp.sum(block_start[:, None] >= padded_end[None, :], axis=1), N_EXPERTS - 1)
    x_rows = xf[row_token].reshape(n_blocks, MOE_BLOCK, d)

    def expert_block(args):
        xb, e = args
        h = jax.nn.silu(xb @ w_gate[e]) * (xb @ w_up[e])
        return h @ w_down[e]

    y_rows = lax.map(expert_block, (x_rows, block_expert)).reshape(n_rows, d)
    y = jax.ops.segment_sum(y_rows * row_gate[:, None], row_token, num_segments=n_tok)
    return y.reshape(bsz, seq, d)


def setup_inputs(seed: int = 0) -> dict:
    key = jax.random.key(seed)
    ks = jax.random.split(key, 16)
    nrm = jax.random.normal
    f32 = jnp.float32
    return {
        "x": nrm(ks[0], (BATCH, SEQ, D_MODEL), f32),
        "w_in": nrm(ks[1], (DEPTH, D_MODEL, D_IN), f32) * D_MODEL ** -0.5,
        "w_branch": nrm(ks[2], (DEPTH, N_BRANCHES, MIX_WIDTH, D_MODEL), f32) * MIX_WIDTH ** -0.5,
        "w_out": nrm(ks[3], (DEPTH, D_MODEL, D_MODEL), f32) * (D_MODEL ** -0.5 * DEEPNORM_BETA),
        "fox_fgate_bias": 0.01 * nrm(ks[4], (DEPTH, N_HEADS), f32),
        "hgrn_lb_logits": 0.1 * nrm(ks[5], (DEPTH, MIX_WIDTH), f32),
        "ln1_g": 1.0 + 0.01 * nrm(ks[6], (DEPTH, D_MODEL), f32),
        "ln1_b": 0.01 * nrm(ks[7], (DEPTH, D_MODEL), f32),
        "w_router_group": nrm(ks[8], (DEPTH, D_MODEL, N_GROUPS), f32) * D_MODEL ** -0.5,
        "w_router_expert": nrm(ks[9], (DEPTH, D_MODEL, N_EXPERTS), f32) * D_MODEL ** -0.5,
        "w_up": nrm(ks[10], (DEPTH, N_EXPERTS, D_MODEL, D_FF_EXPERT), f32) * D_MODEL ** -0.5,
        "w_gate": nrm(ks[11], (DEPTH, N_EXPERTS, D_MODEL, D_FF_EXPERT), f32) * D_MODEL ** -0.5,
        "w_down": nrm(ks[12], (DEPTH, N_EXPERTS, D_FF_EXPERT, D_MODEL), f32) * (D_FF_EXPERT ** -0.5 * DEEPNORM_BETA),
        "ln2_g": 1.0 + 0.01 * nrm(ks[13], (DEPTH, D_MODEL), f32),
        "ln2_b": 0.01 * nrm(ks[14], (DEPTH, D_MODEL), f32),
    }


def reference(x, w_in, w_branch, w_out, fox_fgate_bias, hgrn_lb_logits, ln1_g, ln1_b,
              w_router_group, w_router_expert, w_up, w_gate, w_down, ln2_g, ln2_b):
    lb_cum = jnp.cumsum(jax.nn.softmax(hgrn_lb_logits.astype(jnp.float32), axis=0), axis=0)
    lower_bounds = lb_cum - lb_cum[0]
    for layer in range(DEPTH):
        mix = token_mixer(x, w_in[layer], w_branch[layer], w_out[layer],
                          fox_fgate_bias[layer], lower_bounds[layer])
        x = layer_norm(DEEPNORM_ALPHA * x + mix, ln1_g[layer], ln1_b[layer])
        ffn = hierarchical_moe(x, w_router_group[layer], w_router_expert[layer],
                               w_up[layer], w_gate[layer], w_down[layer])
        x = layer_norm(DEEPNORM_ALPHA * x + ffn, ln2_g[layer], ln2_b[layer])
    return x
```

```python
import functools
import math

import numpy as np
import jax
import jax.numpy as jnp
from jax import lax
from jax.experimental import pallas as pl
from jax.experimental.pallas import tpu as pltpu

D_MODEL = 1024
HEAD_DIM = 128
MIX_WIDTH = D_MODEL // 2
N_HEADS = MIX_WIDTH // HEAD_DIM
N_GROUPS = 4
EXPERTS_PER_GROUP = 8
N_EXPERTS = N_GROUPS * EXPERTS_PER_GROUP
TOP_K = 2
D_FF_EXPERT = D_MODEL // 2
LN_EPS = 1e-5
HEAD_NORM_EPS = 1e-6
RET_ROPE_BASE = 10000.0

LANES = 128
HGRN_SUB = 16
RET_CHUNK = 128
MOE_ROWS = 256
VMEM_LIMIT = 48 * 1024 * 1024

COL_GA, COL_GB, COL_GC = 0, 1024, 2048
COL_AQ, COL_AI, COL_AG = 3072, 3584, 4096
COL_BQ, COL_BK, COL_BV = 4608, 5120, 5632
COL_CQ, COL_CK, COL_CV, COL_CG = 6144, 6656, 7168, 7680
N_MAIN = 8192
N_FGATE = MIX_WIDTH + LANES

F32 = jnp.float32
BF16 = jnp.bfloat16


def _cparams(sem):
    return pltpu.CompilerParams(dimension_semantics=sem, vmem_limit_bytes=VMEM_LIMIT)


def _proj_kernel(x_ref, w_ref, o_ref):
    o_ref[...] = jnp.dot(x_ref[...].astype(BF16), w_ref[...],
                         preferred_element_type=F32).astype(o_ref.dtype)


def _project(x2d, w, out_dtype, tm, tn):
    n, d = x2d.shape
    c = w.shape[1]
    return pl.pallas_call(
        _proj_kernel,
        grid=(n // tm, c // tn),
        in_specs=[pl.BlockSpec((tm, d), lambda i, j: (i, 0)),
                  pl.BlockSpec((d, tn), lambda i, j: (0, j))],
        out_specs=pl.BlockSpec((tm, tn), lambda i, j: (i, j)),
        out_shape=jax.ShapeDtypeStruct((n, c), out_dtype),
        compiler_params=_cparams(("parallel", "arbitrary")),
    )(x2d, w)


def _log_sigmoid(z):
    return jnp.minimum(z, 0.0) - jnp.log1p(jnp.exp(-jnp.abs(z)))


def _hgrn2_kernel(q_ref, i_ref, g_ref, f_ref, loglb_ref, log1m_ref, o_ref, st_ref, *, tb):
    @pl.when(pl.program_id(1) == 0)
    def _():
        st_ref[...] = jnp.zeros_like(st_ref)

    row = lax.broadcasted_iota(jnp.int32, (HGRN_SUB, HEAD_DIM), 0)

    def step(j, carry):
        r0 = pl.multiple_of(j * HGRN_SUB, HGRN_SUB)
        rows = pl.ds(r0, HGRN_SUB)
        for h in range(N_HEADS):
            cols = slice(h * HEAD_DIM, (h + 1) * HEAD_DIM)
            z = f_ref[rows, cols]
            a = loglb_ref[:, cols]
            b = log1m_ref[:, cols] + _log_sigmoid(z)
            log_f = jnp.maximum(a, b) + jnp.log1p(jnp.exp(-jnp.abs(a - b)))
            k = 1.0 - jnp.exp(log_f)
            q = q_ref[rows, cols].astype(F32)
            v = i_ref[rows, cols].astype(F32)
            cum = log_f
            for s in (1, 2, 4, 8):
                cum = cum + jnp.where(row >= s, pltpu.roll(cum, s, axis=0), 0.0)
            o = jnp.zeros((HGRN_SUB, HEAD_DIM), F32)
            for s in range(HGRN_SUB):
                dec = jnp.exp(jnp.minimum(cum - cum[s:s + 1, :], 0.0))
                w = jnp.where(row >= s, q * k[s:s + 1, :] * dec, 0.0)
                o = o + jnp.sum(w, axis=1, keepdims=True) * v[s:s + 1, :]
            st = st_ref[h]
            qd = (q * jnp.exp(cum)).astype(BF16)
            o = o + lax.dot_general(qd, st.astype(BF16), (((1,), (1,)), ((), ())),
                                    preferred_element_type=F32)
            last = cum[HGRN_SUB - 1:HGRN_SUB, :]
            kd = (k * jnp.exp(last - cum)).astype(BF16)
            upd = lax.dot_general(v.astype(BF16), kd, (((0,), (0,)), ((), ())),
                                  preferred_element_type=F32)
            st_ref[h] = st * jnp.exp(last) + upd
            y = o * lax.rsqrt(jnp.mean(o * o, axis=1, keepdims=True) + HEAD_NORM_EPS)
            y = y * jax.nn.sigmoid(g_ref[rows, cols].astype(F32))
            o_ref[rows, cols] = y.astype(o_ref.dtype)
        return carry

    lax.fori_loop(0, tb // HGRN_SUB, step, 0)


def _hgrn2(pb, pf, log_lb, log1m_lb, bsz, seq, tb):
    n = bsz * seq
    nt = seq // tb
    cb = lambda c: c // MIX_WIDTH
    row_map = lambda c: (lambda b, t: (b * nt + t, c))
    return pl.pallas_call(
        functools.partial(_hgrn2_kernel, tb=tb),
        grid=(bsz, nt),
        in_specs=[pl.BlockSpec((tb, MIX_WIDTH), row_map(cb(COL_AQ))),
                  pl.BlockSpec((tb, MIX_WIDTH), row_map(cb(COL_AI))),
                  pl.BlockSpec((tb, MIX_WIDTH), row_map(cb(COL_AG))),
                  pl.BlockSpec((tb, MIX_WIDTH), row_map(0)),
                  pl.BlockSpec((1, MIX_WIDTH), lambda b, t: (0, 0)),
                  pl.BlockSpec((1, MIX_WIDTH), lambda b, t: (0, 0))],
        out_specs=pl.BlockSpec((tb, MIX_WIDTH), row_map(0)),
        out_shape=jax.ShapeDtypeStruct((n, MIX_WIDTH), BF16),
        scratch_shapes=[pltpu.VMEM((N_HEADS, HEAD_DIM, HEAD_DIM), F32)],
        compiler_params=_cparams(("parallel", "arbitrary")),
    )(pb, pb, pb, pf, log_lb, log1m_lb)


def _retention_kernel(q_ref, k_ref, v_ref, g_ref, cos_ref, sin_ref, dmask_ref, qdec_ref, kdec_ref,
                      o_ref, st_ref, *, tb, chunk_decay):
    @pl.when(pl.program_id(1) == 0)
    def _():
        st_ref[...] = jnp.zeros_like(st_ref)

    def step(j, carry):
        r0 = pl.multiple_of(j * RET_CHUNK, RET_CHUNK)
        rows = pl.ds(r0, RET_CHUNK)
        cos = cos_ref[rows, :]
        sin = sin_ref[rows, :]
        for h in range(N_HEADS):
            cols = slice(h * HEAD_DIM, (h + 1) * HEAD_DIM)
            q = q_ref[rows, cols].astype(F32)
            k = k_ref[rows, cols].astype(F32)
            q = q * cos + pltpu.roll(q, HEAD_DIM // 2, axis=1) * sin
            k = (k * cos + pltpu.roll(k, HEAD_DIM // 2, axis=1) * sin) * (HEAD_DIM ** -0.5)
            v = v_ref[rows, cols]
            inner = lax.dot_general(q.astype(BF16), k.astype(BF16), (((1,), (1,)), ((), ())),
                                    preferred_element_type=F32) * dmask_ref[h]
            st = st_ref[h]
            o = (jnp.dot(inner.astype(BF16), v, preferred_element_type=F32)
                 + jnp.dot((q * qdec_ref[h]).astype(BF16), st.astype(BF16),
                           preferred_element_type=F32))
            upd = lax.dot_general((k * kdec_ref[h]).astype(BF16), v, (((0,), (0,)), ((), ())),
                                  preferred_element_type=F32)
            st_ref[h] = st * chunk_decay[h] + upd
            c = o - jnp.mean(o, axis=1, keepdims=True)
            y = c * lax.rsqrt(jnp.mean(c * c, axis=1, keepdims=True) + HEAD_NORM_EPS)
            g = g_ref[rows, cols].astype(F32)
            o_ref[rows, cols] = (y * (g * jax.nn.sigmoid(g))).astype(o_ref.dtype)
        return carry

    lax.fori_loop(0, tb // RET_CHUNK, step, 0)


def _retention_tables(seq):
    half = HEAD_DIM // 2
    inv = 1.0 / (RET_ROPE_BASE ** jnp.linspace(0.0, 1.0, half, dtype=F32))
    ang = jnp.arange(seq, dtype=F32)[:, None] * inv[None, :]
    cos = jnp.cos(ang)
    sin = jnp.sin(ang)
    cos_t = jnp.concatenate([cos, cos], axis=-1)
    sin_t = jnp.concatenate([-sin, sin], axis=-1)
    log_gamma = jnp.log(1.0 - jnp.power(2.0, -5.0 - jnp.arange(N_HEADS, dtype=F32)))
    idx = jnp.arange(RET_CHUNK, dtype=F32)
    rel = idx[:, None] - idx[None, :]
    dmask = jnp.where(rel >= 0, jnp.exp(log_gamma[:, None, None] * jnp.maximum(rel, 0.0)), 0.0)
    ones = jnp.ones((1, 1, HEAD_DIM), F32)
    qdec = jnp.exp(log_gamma[:, None] * (idx + 1.0))[..., None] * ones
    kdec = jnp.exp(log_gamma[:, None] * (RET_CHUNK - 1.0 - idx))[..., None] * ones
    return cos_t, sin_t, dmask, qdec, kdec


def _retention(pb, tables, bsz, seq, tb):
    n = bsz * seq
    nt = seq // tb
    cos_t, sin_t, dmask, qdec, kdec = tables
    chunk_decay = tuple(float((1.0 - 2.0 ** (-5.0 - h)) ** RET_CHUNK) for h in range(N_HEADS))
    cb = lambda c: c // MIX_WIDTH
    row_map = lambda c: (lambda b, t: (b * nt + t, c))
    const3 = lambda b, t: (0, 0, 0)
    return pl.pallas_call(
        functools.partial(_retention_kernel, tb=tb, chunk_decay=chunk_decay),
        grid=(bsz, nt),
        in_specs=[pl.BlockSpec((tb, MIX_WIDTH), row_map(cb(COL_CQ))),
                  pl.BlockSpec((tb, MIX_WIDTH), row_map(cb(COL_CK))),
                  pl.BlockSpec((tb, MIX_WIDTH), row_map(cb(COL_CV))),
                  pl.BlockSpec((tb, MIX_WIDTH), row_map(cb(COL_CG))),
                  pl.BlockSpec((tb, HEAD_DIM), lambda b, t: (t, 0)),
                  pl.BlockSpec((tb, HEAD_DIM), lambda b, t: (t, 0)),
                  pl.BlockSpec((N_HEADS, RET_CHUNK, RET_CHUNK), const3),
                  pl.BlockSpec((N_HEADS, RET_CHUNK, HEAD_DIM), const3),
                  pl.BlockSpec((N_HEADS, RET_CHUNK, HEAD_DIM), const3)],
        out_specs=pl.BlockSpec((tb, MIX_WIDTH), row_map(0)),
        out_shape=jax.ShapeDtypeStruct((n, MIX_WIDTH), BF16),
        scratch_shapes=[pltpu.VMEM((N_HEADS, HEAD_DIM, HEAD_DIM), F32)],
        compiler_params=_cparams(("parallel", "arbitrary")),
    )(pb, pb, pb, pb, cos_t, sin_t, dmask, qdec, kdec)


def _fox_cum_kernel(z_ref, bias_ref, o_ref):
    x = _log_sigmoid(z_ref[...] + bias_ref[...])
    lane = lax.broadcasted_iota(jnp.int32, x.shape, 1)
    s = 1
    while s < x.shape[1]:
        x = x + jnp.where(lane >= s, pltpu.roll(x, s, axis=1), 0.0)
        s *= 2
    o_ref[...] = -x


def _fox_neg_cum(zt, bias_col):
    return pl.pallas_call(
        _fox_cum_kernel,
        out_shape=jax.ShapeDtypeStruct(zt.shape, F32),
        compiler_params=pltpu.CompilerParams(vmem_limit_bytes=VMEM_LIMIT),
    )(zt, bias_col)


def _fox_kernel(qi_ref, kj_ref, q_ref, k_ref, v_ref, nck_ref, o_ref, m_ref, l_ref, acc_ref,
                *, tq, tk, scale):
    p = pl.program_id(2)
    i = qi_ref[p]
    j = kj_ref[p]

    @pl.when(j == 0)
    def _():
        m_ref[...] = jnp.full_like(m_ref, -jnp.inf)
        l_ref[...] = jnp.zeros_like(l_ref)
        acc_ref[...] = jnp.zeros_like(acc_ref)

    s = lax.dot_general(q_ref[...], k_ref[...], (((1,), (1,)), ((), ())),
                        preferred_element_type=F32) * scale + nck_ref[...]
    q_pos = i * tq + lax.broadcasted_iota(jnp.int32, (tq, tk), 0)
    k_pos = j * tk + lax.broadcasted_iota(jnp.int32, (tq, tk), 1)
    s = jnp.where(k_pos <= q_pos, s, -jnp.inf)
    m_prev = m_ref[...]
    m_new = jnp.maximum(m_prev, jnp.max(s, axis=1, keepdims=True))
    alpha = jnp.exp(m_prev - m_new)
    pr = jnp.exp(s - m_new)
    l_ref[...] = alpha * l_ref[...] + jnp.sum(pr, axis=1, keepdims=True)
    acc_ref[...] = alpha * acc_ref[...] + jnp.dot(pr.astype(BF16), v_ref[...],
                                                  preferred_element_type=F32)
    m_ref[...] = m_new

    @pl.when(j == ((i + 1) * tq) // tk - 1)
    def _():
        o_ref[...] = (acc_ref[...] / l_ref[...]).astype(o_ref.dtype)


def _fox(pb, neg_cum, bsz, seq, tq, tk):
    n = bsz * seq
    nq = seq // tq
    qi, kj = [], []
    for i in range(nq):
        for j in range(((i + 1) * tq) // tk):
            qi.append(i)
            kj.append(j)
    qi = jnp.asarray(np.array(qi, np.int32))
    kj = jnp.asarray(np.array(kj, np.int32))
    cq, ck, cv = COL_BQ // HEAD_DIM, COL_BK // HEAD_DIM, COL_BV // HEAD_DIM
    nkb = seq // tk
    grid_spec = pltpu.PrefetchScalarGridSpec(
        num_scalar_prefetch=2,
        grid=(bsz, N_HEADS, len(qi)),
        in_specs=[pl.BlockSpec((tq, HEAD_DIM), lambda b, h, p, qi, kj: (b * nq + qi[p], cq + h)),
                  pl.BlockSpec((tk, HEAD_DIM), lambda b, h, p, qi, kj: (b * nkb + kj[p], ck + h)),
                  pl.BlockSpec((tk, HEAD_DIM), lambda b, h, p, qi, kj: (b * nkb + kj[p], cv + h)),
                  pl.BlockSpec((None, None, 1, tk), lambda b, h, p, qi, kj: (b, h, 0, kj[p]))],
        out_specs=pl.BlockSpec((tq, HEAD_DIM), lambda b, h, p, qi, kj: (b * nq + qi[p], h)),
        scratch_shapes=[pltpu.VMEM((tq, 1), F32), pltpu.VMEM((tq, 1), F32),
                        pltpu.VMEM((tq, HEAD_DIM), F32)],
    )
    return pl.pallas_call(
        functools.partial(_fox_kernel, tq=tq, tk=tk, scale=HEAD_DIM ** -0.5),
        grid_spec=grid_spec,
        out_shape=jax.ShapeDtypeStruct((n, MIX_WIDTH), BF16),
        compiler_params=_cparams(("parallel", "parallel", "arbitrary")),
    )(qi, kj, pb, pb, pb, neg_cum)


def _layer_norm(x, g, b):
    mu = jnp.mean(x, axis=1, keepdims=True)
    c = x - mu
    var = jnp.mean(c * c, axis=1, keepdims=True)
    return c * lax.rsqrt(var + LN_EPS) * g + b


def _route(logits):
    lane = lax.broadcasted_iota(jnp.int32, logits.shape, 1)
    lane_f = lane.astype(F32)
    ninf = -jnp.inf
    big = float(LANES)
    gl = jnp.where(lane < N_GROUPS, logits, ninf)
    gmax = jnp.max(gl, axis=1, keepdims=True)
    gidx = jnp.min(jnp.where(gl == gmax, lane_f, big), axis=1, keepdims=True)
    gprob = 1.0 / jnp.sum(jnp.exp(gl - gmax), axis=1, keepdims=True)
    e_group = ((lane - N_GROUPS) // EXPERTS_PER_GROUP).astype(F32)
    in_grp = (lane >= N_GROUPS) & (lane < N_GROUPS + N_EXPERTS) & (e_group == gidx)
    el = jnp.where(in_grp, logits, ninf)
    t1 = jnp.max(el, axis=1, keepdims=True)
    i1 = jnp.min(jnp.where(el == t1, lane_f, big), axis=1, keepdims=True)
    el2 = jnp.where(lane_f == i1, ninf, el)
    t2 = jnp.max(el2, axis=1, keepdims=True)
    i2 = jnp.min(jnp.where(el2 == t2, lane_f, big), axis=1, keepdims=True)
    d = jnp.exp(t2 - t1)
    g1 = gprob / (1.0 + d)
    g2 = gprob * d / (1.0 + d)
    out = jnp.where(lane == 0, i1 - N_GROUPS, 0.0)
    out = jnp.where(lane == 1, i2 - N_GROUPS, out)
    out = jnp.where(lane == 2, g1, out)
    out = jnp.where(lane == 3, g2, out)
    return out


def _merge_kernel(x_ref, ya_ref, yb_ref, yc_ref, ga_ref, gb_ref, gc_ref, wb_ref, wo_ref,
                  lng_ref, lnb_ref, wr_ref, x1_ref, route_ref, *, alpha):
    def branch(y_ref, g_ref, idx):
        return jax.nn.sigmoid(g_ref[...].astype(F32)) * jnp.dot(
            y_ref[...], wb_ref[idx], preferred_element_type=F32)

    merged = branch(ya_ref, ga_ref, 0) + branch(yb_ref, gb_ref, 1) + branch(yc_ref, gc_ref, 2)
    mix = jnp.dot(merged.astype(BF16), wo_ref[...], preferred_element_type=F32)
    x1 = _layer_norm(alpha * x_ref[...] + mix, lng_ref[...], lnb_ref[...])
    x1_ref[...] = x1
    logits = jnp.dot(x1, wr_ref[...], preferred_element_type=F32, precision=lax.Precision.HIGHEST)
    route_ref[...] = _route(logits)


def _merge(x2d, ya, yb, yc, pb, wb, wo, lng, lnb, wr, alpha, tm):
    n = x2d.shape[0]
    row = lambda c: (lambda i: (i, c))
    const2 = lambda i: (0, 0)
    return pl.pallas_call(
        functools.partial(_merge_kernel, alpha=alpha),
        grid=(n // tm,),
        in_specs=[pl.BlockSpec((tm, D_MODEL), row(0)),
                  pl.BlockSpec((tm, MIX_WIDTH), row(0)),
                  pl.BlockSpec((tm, MIX_WIDTH), row(0)),
                  pl.BlockSpec((tm, MIX_WIDTH), row(0)),
                  pl.BlockSpec((tm, D_MODEL), row(COL_GA // D_MODEL)),
                  pl.BlockSpec((tm, D_MODEL), row(COL_GB // D_MODEL)),
                  pl.BlockSpec((tm, D_MODEL), row(COL_GC // D_MODEL)),
                  pl.BlockSpec((3, MIX_WIDTH, D_MODEL), lambda i: (0, 0, 0)),
                  pl.BlockSpec((D_MODEL, D_MODEL), const2),
                  pl.BlockSpec((1, D_MODEL), const2),
                  pl.BlockSpec((1, D_MODEL), const2),
                  pl.BlockSpec((D_MODEL, LANES), const2)],
        out_specs=[pl.BlockSpec((tm, D_MODEL), row(0)),
                   pl.BlockSpec((tm, LANES), row(0))],
        out_shape=[jax.ShapeDtypeStruct((n, D_MODEL), F32),
                   jax.ShapeDtypeStruct((n, LANES), F32)],
        compiler_params=_cparams(("parallel",)),
    )(x2d, ya, yb, yc, pb, pb, pb, wb, wo, lng, lnb, wr)


def _moe_kernel(bexp_ref, tok_ref, dst_ref, x_hbm, wg_ref, wu_ref, wd_ref, y_hbm,
                xbuf, ybuf, sem_in, sem_out):
    del bexp_ref

    def row_in(r, tok):
        return pltpu.make_async_copy(x_hbm.at[pl.ds(tok, 1), :], xbuf.at[pl.ds(r, 1), :], sem_in)

    def row_out(r, dst):
        return pltpu.make_async_copy(ybuf.at[pl.ds(r, 1), :], y_hbm.at[pl.ds(dst, 1), :], sem_out)

    def gather(r, c):
        row_in(r, tok_ref[0, 0, r]).start()
        return c

    lax.fori_loop(0, MOE_ROWS, gather, 0)

    def gather_wait(r, c):
        row_in(r, 0).wait()
        return c

    lax.fori_loop(0, MOE_ROWS, gather_wait, 0)

    xb = xbuf[...].astype(BF16)
    hg = jnp.dot(xb, wg_ref[0], preferred_element_type=F32)
    hu = jnp.dot(xb, wu_ref[0], preferred_element_type=F32)
    hid = (hg * jax.nn.sigmoid(hg)) * hu
    ybuf[...] = jnp.dot(hid.astype(BF16), wd_ref[0], preferred_element_type=F32)

    def scatter(r, c):
        row_out(r, dst_ref[0, 0, r]).start()
        return c

    lax.fori_loop(0, MOE_ROWS, scatter, 0)

    def scatter_wait(r, c):
        row_out(r, 0).wait()
        return c

    lax.fori_loop(0, MOE_ROWS, scatter_wait, 0)


def _moe(x1, block_expert, row_token, row_dst, wg, wu, wd, n_out_rows):
    n_blocks = block_expert.shape[0]
    idx_spec = pl.BlockSpec((1, 1, MOE_ROWS), lambda i, be: (i, 0, 0), memory_space=pltpu.SMEM)
    grid_spec = pltpu.PrefetchScalarGridSpec(
        num_scalar_prefetch=1,
        grid=(n_blocks,),
        in_specs=[idx_spec, idx_spec,
                  pl.BlockSpec(memory_space=pl.ANY),
                  pl.BlockSpec((1, D_MODEL, D_FF_EXPERT), lambda i, be: (be[i], 0, 0)),
                  pl.BlockSpec((1, D_MODEL, D_FF_EXPERT), lambda i, be: (be[i], 0, 0)),
                  pl.BlockSpec((1, D_FF_EXPERT, D_MODEL), lambda i, be: (be[i], 0, 0))],
        out_specs=pl.BlockSpec(memory_space=pl.ANY),
        scratch_shapes=[pltpu.VMEM((MOE_ROWS, D_MODEL), F32),
                        pltpu.VMEM((MOE_ROWS, D_MODEL), F32),
                        pltpu.SemaphoreType.DMA, pltpu.SemaphoreType.DMA],
    )
    return pl.pallas_call(
        _moe_kernel,
        grid_spec=grid_spec,
        out_shape=jax.ShapeDtypeStruct((n_out_rows, D_MODEL), F32),
        compiler_params=_cparams(("arbitrary",)),
    )(block_expert, row_token.reshape(n_blocks, 1, MOE_ROWS), row_dst.reshape(n_blocks, 1, MOE_ROWS),
      x1, wg, wu, wd)


def _dispatch_plan(route, n_tok):
    n_assign = n_tok * TOP_K
    expert_id = route[:, :TOP_K].astype(jnp.int32).reshape(-1)
    order = jnp.argsort(expert_id)
    sorted_e = expert_id[order]
    counts = jnp.sum(expert_id[:, None] == jnp.arange(N_EXPERTS)[None, :], axis=0, dtype=jnp.int32)
    starts = jnp.cumsum(counts) - counts
    padded = (counts + MOE_ROWS - 1) // MOE_ROWS * MOE_ROWS
    padded_end = jnp.cumsum(padded)
    padded_start = padded_end - padded
    dest = padded_start[sorted_e] + jnp.arange(n_assign, dtype=jnp.int32) - starts[sorted_e]
    n_rows = n_assign + N_EXPERTS * MOE_ROWS
    n_blocks = n_rows // MOE_ROWS
    row_assign = jnp.full((n_rows,), -1, jnp.int32).at[dest].set(order.astype(jnp.int32))
    is_pad = row_assign < 0
    pad_rank = jnp.cumsum(is_pad.astype(jnp.int32)) - 1
    tok = row_assign // TOP_K
    slot = row_assign % TOP_K
    row_token = jnp.where(is_pad, 0, tok)
    row_dst = jnp.where(is_pad, n_assign + pad_rank, slot * n_tok + tok)
    block_start = jnp.arange(n_blocks, dtype=jnp.int32) * MOE_ROWS
    block_expert = jnp.minimum(jnp.sum(block_start[:, None] >= padded_end[None, :], axis=1),
                               N_EXPERTS - 1).astype(jnp.int32)
    return block_expert, row_token, row_dst, n_rows


def _combine_kernel(x_ref, y0_ref, y1_ref, route_ref, lng_ref, lnb_ref, o_ref, *, alpha):
    r = route_ref[...]
    ffn = r[:, 2:3] * y0_ref[...] + r[:, 3:4] * y1_ref[...]
    o_ref[...] = _layer_norm(alpha * x_ref[...] + ffn, lng_ref[...], lnb_ref[...])


def _combine(x1, y2, route, lng, lnb, alpha, tm):
    n = x1.shape[0]
    nb = n // tm
    const2 = lambda i: (0, 0)
    return pl.pallas_call(
        functools.partial(_combine_kernel, alpha=alpha),
        grid=(nb,),
        in_specs=[pl.BlockSpec((tm, D_MODEL), lambda i: (i, 0)),
                  pl.BlockSpec((tm, D_MODEL), lambda i: (i, 0)),
                  pl.BlockSpec((tm, D_MODEL), lambda i: (nb + i, 0)),
                  pl.BlockSpec((tm, LANES), lambda i: (i, 0)),
                  pl.BlockSpec((1, D_MODEL), const2),
                  pl.BlockSpec((1, D_MODEL), const2)],
        out_specs=pl.BlockSpec((tm, D_MODEL), lambda i: (i, 0)),
        out_shape=jax.ShapeDtypeStruct((n, D_MODEL), F32),
        compiler_params=_cparams(("parallel",)),
    )(x1, y2, y2, route, lng, lnb)


def _permute_w_in(w):
    sizes = (MIX_WIDTH,) * 4 + (MIX_WIDTH,) * 3 + (N_HEADS,) + (MIX_WIDTH,) * 4 + (D_MODEL,) * 3
    offs = np.concatenate([[0], np.cumsum(sizes)])
    (a_q, a_f, a_i, a_g, b_q, b_k, b_v, b_f, c_q, c_k, c_v, c_g, g_a, g_b, g_c) = [
        w[:, offs[t]:offs[t + 1]] for t in range(len(sizes))]
    main = jnp.concatenate([g_a, g_b, g_c, a_q, a_i, a_g, b_q, b_k, b_v, c_q, c_k, c_v, c_g], axis=1)
    fpart = jnp.concatenate([a_f, b_f, jnp.zeros((w.shape[0], LANES - N_HEADS), w.dtype)], axis=1)
    return main.astype(BF16), fpart.astype(BF16)


def kernel(x, w_in, w_branch, w_out, fox_fgate_bias, hgrn_lb_logits, ln1_g, ln1_b,
           w_router_group, w_router_expert, w_up, w_gate, w_down, ln2_g, ln2_b):
    bsz, seq, d = x.shape
    depth = w_in.shape[0]
    n = bsz * seq
    alpha = float((2 * depth) ** 0.25)
    tm = min(1024, n)
    tb = min(512, seq)
    tq = min(512, seq)

    lb_cum = jnp.cumsum(jax.nn.softmax(hgrn_lb_logits.astype(F32), axis=0), axis=0)
    lower_bounds = lb_cum - lb_cum[0]
    tables = _retention_tables(seq)

    h = x.reshape(n, d)
    for layer in range(depth):
        w_main, w_f = _permute_w_in(w_in[layer])
        pb = _project(h, w_main, BF16, tm, 1024)
        pf = _project(h, w_f, F32, tm, N_FGATE)

        lb = lower_bounds[layer][None, :]
        ya = _hgrn2(pb, pf, jnp.log(lb), jnp.log1p(-lb), bsz, seq, tb)

        zt = pf[:, MIX_WIDTH:MIX_WIDTH + N_HEADS].reshape(bsz, seq, N_HEADS).transpose(0, 2, 1)
        bias_col = jnp.tile(fox_fgate_bias[layer].astype(F32), bsz)[:, None]
        neg_cum = _fox_neg_cum(zt.reshape(bsz * N_HEADS, seq), bias_col)
        yb = _fox(pb, neg_cum.reshape(bsz, N_HEADS, 1, seq), bsz, seq, tq, tq)

        yc = _retention(pb, tables, bsz, seq, tb)

        w_route = jnp.concatenate(
            [w_router_group[layer], w_router_expert[layer],
             jnp.zeros((d, LANES - N_GROUPS - N_EXPERTS), F32)], axis=1)
        x1, route = _merge(h, ya, yb, yc, pb, w_branch[layer].astype(BF16), w_out[layer].astype(BF16),
                           ln1_g[layer][None, :], ln1_b[layer][None, :], w_route, alpha, min(512, n))

        block_expert, row_token, row_dst, n_rows = _dispatch_plan(route, n)
        y2 = _moe(x1, block_expert, row_token, row_dst, w_gate[layer].astype(BF16),
                  w_up[layer].astype(BF16), w_down[layer].astype(BF16), n_rows)
        h = _combine(x1, y2, route, ln2_g[layer][None, :], ln2_b[layer][None, :], alpha, min(1024, n))
    return h.reshape(bsz, seq, d)
```

```python
import functools
import math

import numpy as np
import jax
import jax.numpy as jnp
from jax import lax
from jax.experimental import pallas as pl
from jax.experimental.pallas import tpu as pltpu

D_MODEL = 1024
HEAD_DIM = 128
MIX_WIDTH = D_MODEL // 2
N_HEADS = MIX_WIDTH // HEAD_DIM
N_GROUPS = 4
EXPERTS_PER_GROUP = 8
N_EXPERTS = N_GROUPS * EXPERTS_PER_GROUP
TOP_K = 2
D_FF_EXPERT = D_MODEL // 2
LN_EPS = 1e-5
HEAD_NORM_EPS = 1e-6
RET_ROPE_BASE = 10000.0

LANES = 128
SUBLANES = 8
HGRN_SUB = 2 * SUBLANES
RET_CHUNK = 128
FOX_BLOCK = 512
MOE_ROWS = 512
DMA_UNROLL = 8
VMEM_LIMIT = 48 * 1024 * 1024
LOG2E = 1.4426950408889634

COL_GA, COL_GB, COL_GC = 0, 1024, 2048
COL_AQ, COL_AI, COL_AG = 3072, 3584, 4096
COL_BQ, COL_BK, COL_BV = 4608, 5120, 5632
COL_CQ, COL_CK, COL_CV, COL_CG = 6144, 6656, 7168, 7680
N_MAIN = 8192
N_FGATE = MIX_WIDTH + LANES

F32 = jnp.float32
BF16 = jnp.bfloat16


def _cparams(sem):
    return pltpu.CompilerParams(dimension_semantics=sem, vmem_limit_bytes=VMEM_LIMIT)


def _proj_kernel(x_ref, w_ref, o_ref):
    o_ref[...] = jnp.dot(x_ref[...].astype(BF16), w_ref[...],
                         preferred_element_type=F32).astype(o_ref.dtype)


def _project(x2d, w, out_dtype, tm, tn):
    n, d = x2d.shape
    c = w.shape[1]
    return pl.pallas_call(
        _proj_kernel,
        grid=(n // tm, c // tn),
        in_specs=[pl.BlockSpec((tm, d), lambda i, j: (i, 0)),
                  pl.BlockSpec((d, tn), lambda i, j: (0, j))],
        out_specs=pl.BlockSpec((tm, tn), lambda i, j: (i, j)),
        out_shape=jax.ShapeDtypeStruct((n, c), out_dtype),
        compiler_params=_cparams(("parallel", "arbitrary")),
    )(x2d, w)


def _log_sigmoid(z):
    return jnp.minimum(z, 0.0) - jnp.log1p(jnp.exp(-jnp.abs(z)))


def _hgrn2_kernel(q_ref, i_ref, g_ref, f_ref, loglb_ref, log1m_ref, o_ref, st_ref, *, tb):
    @pl.when(pl.program_id(1) == 0)
    def _():
        st_ref[...] = jnp.zeros_like(st_ref)

    row = lax.broadcasted_iota(jnp.int32, (HGRN_SUB, HEAD_DIM), 0)
    row8 = lax.broadcasted_iota(jnp.int32, (SUBLANES, HEAD_DIM), 0)

    def step(j, carry):
        r0 = pl.multiple_of(j * HGRN_SUB, HGRN_SUB)
        rows = pl.ds(r0, HGRN_SUB)
        for h in range(N_HEADS):
            cols = slice(h * HEAD_DIM, (h + 1) * HEAD_DIM)
            z = f_ref[rows, cols]
            a = loglb_ref[:, cols]
            b = log1m_ref[:, cols] + _log_sigmoid(z)
            log_f = jnp.maximum(a, b) + jnp.log1p(jnp.exp(-jnp.abs(a - b)))
            k = 1.0 - jnp.exp(log_f)
            q = q_ref[rows, cols].astype(F32)
            v = i_ref[rows, cols].astype(F32)
            cum = log_f
            for s in (1, 2, 4, 8):
                cum = cum + jnp.where(row >= s, pltpu.roll(cum, s, axis=0), 0.0)
            q_t, q_b = q[:SUBLANES], q[SUBLANES:]
            c_t, c_b = cum[:SUBLANES], cum[SUBLANES:]
            o_t = jnp.zeros((SUBLANES, HEAD_DIM), F32)
            o_b = jnp.zeros((SUBLANES, HEAD_DIM), F32)
            for s in range(HGRN_SUB):
                cs, ks, vs = cum[s:s + 1, :], k[s:s + 1, :], v[s:s + 1, :]
                if s < SUBLANES:
                    w_b = q_b * ks * jnp.exp(c_b - cs)
                    w_t = q_t * ks * jnp.exp(jnp.minimum(c_t - cs, 0.0))
                    if s > 0:
                        w_t = jnp.where(row8 >= s, w_t, 0.0)
                    o_t = o_t + jnp.sum(w_t, axis=1, keepdims=True) * vs
                else:
                    w_b = q_b * ks * jnp.exp(jnp.minimum(c_b - cs, 0.0))
                    if s > SUBLANES:
                        w_b = jnp.where(row8 >= s - SUBLANES, w_b, 0.0)
                o_b = o_b + jnp.sum(w_b, axis=1, keepdims=True) * vs
            o = jnp.concatenate([o_t, o_b], axis=0)
            st = st_ref[h]
            qd = (q * jnp.exp(cum)).astype(BF16)
            o = o + lax.dot_general(qd, st.astype(BF16), (((1,), (1,)), ((), ())),
                                    preferred_element_type=F32)
            last = cum[HGRN_SUB - 1:HGRN_SUB, :]
            kd = (k * jnp.exp(last - cum)).astype(BF16)
            upd = lax.dot_general(v.astype(BF16), kd, (((0,), (0,)), ((), ())),
                                  preferred_element_type=F32)
            st_ref[h] = st * jnp.exp(last) + upd
            y = o * lax.rsqrt(jnp.mean(o * o, axis=1, keepdims=True) + HEAD_NORM_EPS)
            y = y * jax.nn.sigmoid(g_ref[rows, cols].astype(F32))
            o_ref[rows, cols] = y.astype(o_ref.dtype)
        return carry

    lax.fori_loop(0, tb // HGRN_SUB, step, 0)


def _hgrn2(pb, pf, log_lb, log1m_lb, bsz, seq, tb):
    n = bsz * seq
    nt = seq // tb
    cb = lambda c: c // MIX_WIDTH
    row_map = lambda c: (lambda b, t: (b * nt + t, c))
    return pl.pallas_call(
        functools.partial(_hgrn2_kernel, tb=tb),
        grid=(bsz, nt),
        in_specs=[pl.BlockSpec((tb, MIX_WIDTH), row_map(cb(COL_AQ))),
                  pl.BlockSpec((tb, MIX_WIDTH), row_map(cb(COL_AI))),
                  pl.BlockSpec((tb, MIX_WIDTH), row_map(cb(COL_AG))),
                  pl.BlockSpec((tb, MIX_WIDTH), row_map(0)),
                  pl.BlockSpec((1, MIX_WIDTH), lambda b, t: (0, 0)),
                  pl.BlockSpec((1, MIX_WIDTH), lambda b, t: (0, 0))],
        out_specs=pl.BlockSpec((tb, MIX_WIDTH), row_map(0)),
        out_shape=jax.ShapeDtypeStruct((n, MIX_WIDTH), BF16),
        scratch_shapes=[pltpu.VMEM((N_HEADS, HEAD_DIM, HEAD_DIM), F32)],
        compiler_params=_cparams(("parallel", "arbitrary")),
    )(pb, pb, pb, pf, log_lb, log1m_lb)


def _retention_kernel(q_ref, k_ref, v_ref, g_ref, cos_ref, sin_ref, dmask_ref, qdec_ref, kdec_ref,
                      o_ref, st_ref, *, tb, chunk_decay):
    @pl.when(pl.program_id(1) == 0)
    def _():
        st_ref[...] = jnp.zeros_like(st_ref)

    def step(j, carry):
        r0 = pl.multiple_of(j * RET_CHUNK, RET_CHUNK)
        rows = pl.ds(r0, RET_CHUNK)
        cos = cos_ref[rows, :]
        sin = sin_ref[rows, :]
        for h in range(N_HEADS):
            cols = slice(h * HEAD_DIM, (h + 1) * HEAD_DIM)
            q = q_ref[rows, cols].astype(F32)
            k = k_ref[rows, cols].astype(F32)
            q = q * cos + pltpu.roll(q, HEAD_DIM // 2, axis=1) * sin
            k = (k * cos + pltpu.roll(k, HEAD_DIM // 2, axis=1) * sin) * (HEAD_DIM ** -0.5)
            v = v_ref[rows, cols]
            inner = lax.dot_general(q.astype(BF16), k.astype(BF16), (((1,), (1,)), ((), ())),
                                    preferred_element_type=F32) * dmask_ref[h]
            st = st_ref[h]
            o = (jnp.dot(inner.astype(BF16), v, preferred_element_type=F32)
                 + jnp.dot((q * qdec_ref[h]).astype(BF16), st.astype(BF16),
                           preferred_element_type=F32))
            upd = lax.dot_general((k * kdec_ref[h]).astype(BF16), v, (((0,), (0,)), ((), ())),
                                  preferred_element_type=F32)
            st_ref[h] = st * chunk_decay[h] + upd
            c = o - jnp.mean(o, axis=1, keepdims=True)
            y = c * lax.rsqrt(jnp.mean(c * c, axis=1, keepdims=True) + HEAD_NORM_EPS)
            g = g_ref[rows, cols].astype(F32)
            o_ref[rows, cols] = (y * (g * jax.nn.sigmoid(g))).astype(o_ref.dtype)
        return carry

    lax.fori_loop(0, tb // RET_CHUNK, step, 0)


def _retention_tables(seq):
    half = HEAD_DIM // 2
    inv = 1.0 / (RET_ROPE_BASE ** jnp.linspace(0.0, 1.0, half, dtype=F32))
    ang = jnp.arange(seq, dtype=F32)[:, None] * inv[None, :]
    cos = jnp.cos(ang)
    sin = jnp.sin(ang)
    cos_t = jnp.concatenate([cos, cos], axis=-1)
    sin_t = jnp.concatenate([-sin, sin], axis=-1)
    log_gamma = jnp.log(1.0 - jnp.power(2.0, -5.0 - jnp.arange(N_HEADS, dtype=F32)))
    idx = jnp.arange(RET_CHUNK, dtype=F32)
    rel = idx[:, None] - idx[None, :]
    dmask = jnp.where(rel >= 0, jnp.exp(log_gamma[:, None, None] * jnp.maximum(rel, 0.0)), 0.0)
    ones = jnp.ones((1, 1, HEAD_DIM), F32)
    qdec = jnp.exp(log_gamma[:, None] * (idx + 1.0))[..., None] * ones
    kdec = jnp.exp(log_gamma[:, None] * (RET_CHUNK - 1.0 - idx))[..., None] * ones
    return cos_t, sin_t, dmask, qdec, kdec


def _retention(pb, tables, bsz, seq, tb):
    n = bsz * seq
    nt = seq // tb
    cos_t, sin_t, dmask, qdec, kdec = tables
    chunk_decay = tuple(float((1.0 - 2.0 ** (-5.0 - h)) ** RET_CHUNK) for h in range(N_HEADS))
    cb = lambda c: c // MIX_WIDTH
    row_map = lambda c: (lambda b, t: (b * nt + t, c))
    const3 = lambda b, t: (0, 0, 0)
    return pl.pallas_call(
        functools.partial(_retention_kernel, tb=tb, chunk_decay=chunk_decay),
        grid=(bsz, nt),
        in_specs=[pl.BlockSpec((tb, MIX_WIDTH), row_map(cb(COL_CQ))),
                  pl.BlockSpec((tb, MIX_WIDTH), row_map(cb(COL_CK))),
                  pl.BlockSpec((tb, MIX_WIDTH), row_map(cb(COL_CV))),
                  pl.BlockSpec((tb, MIX_WIDTH), row_map(cb(COL_CG))),
                  pl.BlockSpec((tb, HEAD_DIM), lambda b, t: (t, 0)),
                  pl.BlockSpec((tb, HEAD_DIM), lambda b, t: (t, 0)),
                  pl.BlockSpec((N_HEADS, RET_CHUNK, RET_CHUNK), const3),
                  pl.BlockSpec((N_HEADS, RET_CHUNK, HEAD_DIM), const3),
                  pl.BlockSpec((N_HEADS, RET_CHUNK, HEAD_DIM), const3)],
        out_specs=pl.BlockSpec((tb, MIX_WIDTH), row_map(0)),
        out_shape=jax.ShapeDtypeStruct((n, MIX_WIDTH), BF16),
        scratch_shapes=[pltpu.VMEM((N_HEADS, HEAD_DIM, HEAD_DIM), F32)],
        compiler_params=_cparams(("parallel", "arbitrary")),
    )(pb, pb, pb, pb, cos_t, sin_t, dmask, qdec, kdec)


def _fox_cum_kernel(z_ref, bias_ref, o_ref):
    z = z_ref[...]
    hi = z.astype(BF16)
    rem = z - hi.astype(F32)
    mid = rem.astype(BF16)
    lo = (rem - mid.astype(F32)).astype(BF16)
    sel = jnp.where(lax.broadcasted_iota(jnp.int32, (SUBLANES, LANES), 0)
                    == lax.broadcasted_iota(jnp.int32, (SUBLANES, LANES), 1), 1.0, 0.0).astype(BF16)
    nt = (((1,), (1,)), ((), ()))
    zt = (lax.dot_general(sel, hi, nt, preferred_element_type=F32)
          + lax.dot_general(sel, mid, nt, preferred_element_type=F32)
          + lax.dot_general(sel, lo, nt, preferred_element_type=F32))
    x = _log_sigmoid(zt + bias_ref[...])
    lane = lax.broadcasted_iota(jnp.int32, x.shape, 1)
    s = 1
    while s < x.shape[1]:
        x = x + jnp.where(lane >= s, pltpu.roll(x, s, axis=1), 0.0)
        s *= 2
    o_ref[...] = x * (-LOG2E)


def _fox_neg_cum(pf, bias_col, bsz, seq):
    return pl.pallas_call(
        _fox_cum_kernel,
        grid=(bsz,),
        in_specs=[pl.BlockSpec((seq, LANES), lambda b: (b, MIX_WIDTH // LANES)),
                  pl.BlockSpec((SUBLANES, 1), lambda b: (0, 0))],
        out_specs=pl.BlockSpec((None, SUBLANES, seq), lambda b: (b, 0, 0)),
        out_shape=jax.ShapeDtypeStruct((bsz, SUBLANES, seq), F32),
        compiler_params=_cparams(("parallel",)),
    )(pf, bias_col)


def _fox_kernel(q_ref, k_ref, v_ref, nck_ref, o_ref, m_ref, acc_ref, *, nsub, scale):
    i = pl.program_id(2)
    tk = FOX_BLOCK
    m_ref[...] = jnp.full_like(m_ref, -jnp.inf)
    acc_ref[...] = jnp.zeros_like(acc_ref)
    qs = [(q_ref[a * tk:(a + 1) * tk, :].astype(F32) * (scale * LOG2E)).astype(BF16)
          for a in range(nsub)]
    ones = jnp.ones((tk, HEAD_DIM), BF16)
    causal = (lax.broadcasted_iota(jnp.int32, (tk, tk), 1)
              <= lax.broadcasted_iota(jnp.int32, (tk, tk), 0))

    def attend(a, kblk, masked):
        rows = pl.ds(pl.multiple_of(kblk * tk, tk), tk)
        s = lax.dot_general(qs[a], k_ref[rows, :], (((1,), (1,)), ((), ())),
                            preferred_element_type=F32) + nck_ref[kblk]
        if masked:
            s = jnp.where(causal, s, -jnp.inf)
        m_prev = m_ref[a]
        m_new = jnp.maximum(m_prev, jnp.max(s, axis=1, keepdims=True))
        p = jnp.exp2(s - m_new).astype(BF16)
        v_aug = jnp.concatenate([v_ref[rows, :], ones], axis=1)
        acc_ref[a] = jnp.exp2(m_prev - m_new) * acc_ref[a] + jnp.dot(
            p, v_aug, preferred_element_type=F32)
        m_ref[a] = m_new

    def below_diagonal(j, c):
        for a in range(nsub):
            attend(a, j, False)
        return c

    lax.fori_loop(0, i * nsub, below_diagonal, 0)
    for d in range(nsub):
        for a in range(d, nsub):
            attend(a, i * nsub + d, a == d)
    for a in range(nsub):
        acc = acc_ref[a]
        o_ref[a * tk:(a + 1) * tk, :] = (acc[:, :HEAD_DIM] / acc[:, HEAD_DIM:HEAD_DIM + 1]).astype(o_ref.dtype)


def _fox(pb, neg_cum, bsz, seq, nsub):
    n = bsz * seq
    tq = nsub * FOX_BLOCK
    nq = seq // tq
    nkb = seq // FOX_BLOCK
    cq, ck, cv = COL_BQ // HEAD_DIM, COL_BK // HEAD_DIM, COL_BV // HEAD_DIM
    return pl.pallas_call(
        functools.partial(_fox_kernel, nsub=nsub, scale=HEAD_DIM ** -0.5),
        grid=(bsz, N_HEADS, nq),
        in_specs=[pl.BlockSpec((tq, HEAD_DIM), lambda b, h, i: (b * nq + i, cq + h)),
                  pl.BlockSpec((seq, HEAD_DIM), lambda b, h, i: (b, ck + h)),
                  pl.BlockSpec((seq, HEAD_DIM), lambda b, h, i: (b, cv + h)),
                  pl.BlockSpec((None, None, nkb, 1, FOX_BLOCK), lambda b, h, i: (b, h, 0, 0, 0))],
        out_specs=pl.BlockSpec((tq, HEAD_DIM), lambda b, h, i: (b * nq + i, h)),
        out_shape=jax.ShapeDtypeStruct((n, MIX_WIDTH), BF16),
        scratch_shapes=[pltpu.VMEM((nsub, FOX_BLOCK, 1), F32),
                        pltpu.VMEM((nsub, FOX_BLOCK, 2 * HEAD_DIM), F32)],
        compiler_params=_cparams(("parallel", "parallel", "arbitrary")),
    )(pb, pb, pb, neg_cum.reshape(bsz, SUBLANES, nkb, 1, FOX_BLOCK))


def _layer_norm(x, g, b):
    mu = jnp.mean(x, axis=1, keepdims=True)
    c = x - mu
    var = jnp.mean(c * c, axis=1, keepdims=True)
    return c * lax.rsqrt(var + LN_EPS) * g + b


def _route(logits):
    lane = lax.broadcasted_iota(jnp.int32, logits.shape, 1)
    lane_f = lane.astype(F32)
    ninf = -jnp.inf
    big = float(LANES)
    gl = jnp.where(lane < N_GROUPS, logits, ninf)
    gmax = jnp.max(gl, axis=1, keepdims=True)
    gidx = jnp.min(jnp.where(gl == gmax, lane_f, big), axis=1, keepdims=True)
    gprob = 1.0 / jnp.sum(jnp.exp(gl - gmax), axis=1, keepdims=True)
    e_group = ((lane - N_GROUPS) // EXPERTS_PER_GROUP).astype(F32)
    in_grp = (lane >= N_GROUPS) & (lane < N_GROUPS + N_EXPERTS) & (e_group == gidx)
    el = jnp.where(in_grp, logits, ninf)
    t1 = jnp.max(el, axis=1, keepdims=True)
    i1 = jnp.min(jnp.where(el == t1, lane_f, big), axis=1, keepdims=True)
    el2 = jnp.where(lane_f == i1, ninf, el)
    t2 = jnp.max(el2, axis=1, keepdims=True)
    i2 = jnp.min(jnp.where(el2 == t2, lane_f, big), axis=1, keepdims=True)
    d = jnp.exp(t2 - t1)
    g1 = gprob / (1.0 + d)
    g2 = gprob * d / (1.0 + d)
    out = jnp.where(lane == 0, i1 - N_GROUPS, 0.0)
    out = jnp.where(lane == 1, i2 - N_GROUPS, out)
    out = jnp.where(lane == 2, g1, out)
    out = jnp.where(lane == 3, g2, out)
    return out


def _merge_kernel(x_ref, ya_ref, yb_ref, yc_ref, ga_ref, gb_ref, gc_ref, wb_ref, wo_ref,
                  lng_ref, lnb_ref, wrh_ref, wrl_ref, x1_ref, route_ref, *, alpha):
    def branch(y_ref, g_ref, idx):
        return jax.nn.sigmoid(g_ref[...].astype(F32)) * jnp.dot(
            y_ref[...], wb_ref[idx], preferred_element_type=F32)

    merged = branch(ya_ref, ga_ref, 0) + branch(yb_ref, gb_ref, 1) + branch(yc_ref, gc_ref, 2)
    mix = jnp.dot(merged.astype(BF16), wo_ref[...], preferred_element_type=F32)
    x1 = _layer_norm(alpha * x_ref[...] + mix, lng_ref[...], lnb_ref[...])
    x1_ref[...] = x1
    x_hi = x1.astype(BF16)
    x_lo = (x1 - x_hi.astype(F32)).astype(BF16)
    logits = (jnp.dot(x_hi, wrh_ref[...], preferred_element_type=F32)
              + jnp.dot(x_lo, wrh_ref[...], preferred_element_type=F32)
              + jnp.dot(x_hi, wrl_ref[...], preferred_element_type=F32))
    route_ref[...] = _route(logits)


def _merge(x2d, ya, yb, yc, pb, wb, wo, lng, lnb, wr_hi, wr_lo, alpha, tm):
    n = x2d.shape[0]
    row = lambda c: (lambda i: (i, c))
    const2 = lambda i: (0, 0)
    return pl.pallas_call(
        functools.partial(_merge_kernel, alpha=alpha),
        grid=(n // tm,),
        in_specs=[pl.BlockSpec((tm, D_MODEL), row(0)),
                  pl.BlockSpec((tm, MIX_WIDTH), row(0)),
                  pl.BlockSpec((tm, MIX_WIDTH), row(0)),
                  pl.BlockSpec((tm, MIX_WIDTH), row(0)),
                  pl.BlockSpec((tm, D_MODEL), row(COL_GA // D_MODEL)),
                  pl.BlockSpec((tm, D_MODEL), row(COL_GB // D_MODEL)),
                  pl.BlockSpec((tm, D_MODEL), row(COL_GC // D_MODEL)),
                  pl.BlockSpec((3, MIX_WIDTH, D_MODEL), lambda i: (0, 0, 0)),
                  pl.BlockSpec((D_MODEL, D_MODEL), const2),
                  pl.BlockSpec((1, D_MODEL), const2),
                  pl.BlockSpec((1, D_MODEL), const2),
                  pl.BlockSpec((D_MODEL, LANES), const2),
                  pl.BlockSpec((D_MODEL, LANES), const2)],
        out_specs=[pl.BlockSpec((tm, D_MODEL), row(0)),
                   pl.BlockSpec((tm, LANES), row(0))],
        out_shape=[jax.ShapeDtypeStruct((n, D_MODEL), F32),
                   jax.ShapeDtypeStruct((n, LANES), F32)],
        compiler_params=_cparams(("parallel",)),
    )(x2d, ya, yb, yc, pb, pb, pb, wb, wo, lng, lnb, wr_hi, wr_lo)


def _rank_kernel(route_ref, rank_ref, cnt_ref, carry_ref, *, tr):
    @pl.when(pl.program_id(0) == 0)
    def _():
        carry_ref[...] = jnp.zeros_like(carry_ref)

    r = route_ref[...]
    lane = lax.broadcasted_iota(jnp.int32, r.shape, 1)
    lane_f = lane.astype(F32)
    oh1 = lane_f == r[:, 0:1]
    oh2 = lane_f == r[:, 1:2]
    oh = jnp.where(oh1, 1.0, jnp.where(oh2, 1.0, 0.0))
    earlier = jnp.where(lax.broadcasted_iota(jnp.int32, (tr, tr), 1)
                        < lax.broadcasted_iota(jnp.int32, (tr, tr), 0), 1.0, 0.0).astype(BF16)
    before = jnp.dot(earlier, oh.astype(BF16), preferred_element_type=F32) + carry_ref[...]
    rank1 = jnp.sum(jnp.where(oh1, before, 0.0), axis=1, keepdims=True)
    rank2 = jnp.sum(jnp.where(oh2, before, 0.0), axis=1, keepdims=True)
    rank_ref[...] = jnp.where(lane == 0, rank1, jnp.where(lane == 1, rank2, 0.0))
    total = carry_ref[...] + jnp.sum(oh, axis=0, keepdims=True)
    carry_ref[...] = total
    cnt_ref[...] = total


def _rank(route, tr):
    n = route.shape[0]
    return pl.pallas_call(
        functools.partial(_rank_kernel, tr=tr),
        grid=(n // tr,),
        in_specs=[pl.BlockSpec((tr, LANES), lambda i: (i, 0))],
        out_specs=[pl.BlockSpec((tr, LANES), lambda i: (i, 0)),
                   pl.BlockSpec((1, LANES), lambda i: (0, 0))],
        out_shape=[jax.ShapeDtypeStruct((n, LANES), F32),
                   jax.ShapeDtypeStruct((1, LANES), F32)],
        scratch_shapes=[pltpu.VMEM((1, LANES), F32)],
        compiler_params=_cparams(("arbitrary",)),
    )(route)


def _dispatch_plan(route, rank, counts, n_tok):
    n_rows = n_tok * TOP_K + N_EXPERTS * MOE_ROWS
    n_blocks = n_rows // MOE_ROWS
    cnt = counts[0, :N_EXPERTS].astype(jnp.int32)
    padded = (cnt + MOE_ROWS - 1) // MOE_ROWS * MOE_ROWS
    padded_end = jnp.cumsum(padded)
    padded_start = padded_end - padded
    expert = route[:, :TOP_K].astype(jnp.int32)
    start = jnp.sum(jnp.where(expert[:, :, None] == jnp.arange(N_EXPERTS)[None, None, :],
                              padded_start[None, None, :], 0), axis=-1)
    dest = start + rank[:, :TOP_K].astype(jnp.int32)
    block_start = jnp.arange(n_blocks, dtype=jnp.int32) * MOE_ROWS
    block_expert = jnp.minimum(jnp.sum(block_start[:, None] >= padded_end[None, :], axis=1),
                               N_EXPERTS - 1).astype(jnp.int32)
    n_used = (padded_end[-1:] // MOE_ROWS).astype(jnp.int32)
    return dest, block_expert, n_used, padded_end.astype(jnp.int32), n_rows


def _dispatch_kernel(pend_ref, d0_ref, d1_ref, x_hbm, xr_hbm, zbuf, sem_z, sem, *, tmb, n_blocks):
    i = pl.program_id(0)

    def zero_copy(e):
        end = pend_ref[e]
        return pltpu.make_async_copy(
            zbuf, xr_hbm.at[pl.ds(pl.multiple_of(end - MOE_ROWS, MOE_ROWS), MOE_ROWS), :], sem_z)

    def nonempty(e):
        return pend_ref[e] > (pend_ref[e - 1] if e > 0 else 0)

    @pl.when(i == 0)
    def _():
        zbuf[...] = jnp.zeros_like(zbuf)
        for e in range(N_EXPERTS):
            @pl.when(nonempty(e))
            def _():
                zero_copy(e).start()
        for e in range(N_EXPERTS):
            @pl.when(nonempty(e))
            def _():
                zero_copy(e).wait()

        def tail_copy(b):
            return pltpu.make_async_copy(
                zbuf, xr_hbm.at[pl.ds(pl.multiple_of(b * MOE_ROWS, MOE_ROWS), MOE_ROWS), :], sem_z)

        def tail_start(b, c):
            tail_copy(b).start()
            return c

        def tail_wait(b, c):
            tail_copy(b).wait()
            return c

        first_unused = pend_ref[N_EXPERTS - 1] // MOE_ROWS
        lax.fori_loop(first_unused, n_blocks, tail_start, 0)
        lax.fori_loop(first_unused, n_blocks, tail_wait, 0)

    base = i * tmb

    def row_copy(t, dst):
        return pltpu.make_async_copy(x_hbm.at[pl.ds(t, 1), :], xr_hbm.at[pl.ds(dst, 1), :], sem)

    def issue(g, c):
        for u in range(DMA_UNROLL):
            r = g * DMA_UNROLL + u
            row_copy(base + r, d0_ref[0, 0, r]).start()
            row_copy(base + r, d1_ref[0, 0, r]).start()
        return c

    lax.fori_loop(0, tmb // DMA_UNROLL, issue, 0)

    for _ in range(TOP_K):
        pltpu.make_async_copy(x_hbm.at[pl.ds(0, tmb), :], xr_hbm.at[pl.ds(0, tmb), :], sem).wait()


def _dispatch(x1, dest, padded_end, n_rows, tmb):
    n = x1.shape[0]
    nb = n // tmb
    idx_spec = pl.BlockSpec((1, 1, tmb), lambda i, pe: (i, 0, 0), memory_space=pltpu.SMEM)
    grid_spec = pltpu.PrefetchScalarGridSpec(
        num_scalar_prefetch=1,
        grid=(nb,),
        in_specs=[idx_spec, idx_spec, pl.BlockSpec(memory_space=pl.ANY)],
        out_specs=pl.BlockSpec(memory_space=pl.ANY),
        scratch_shapes=[pltpu.VMEM((MOE_ROWS, D_MODEL), F32),
                        pltpu.SemaphoreType.DMA, pltpu.SemaphoreType.DMA],
    )
    return pl.pallas_call(
        functools.partial(_dispatch_kernel, tmb=tmb, n_blocks=n_rows // MOE_ROWS),
        grid_spec=grid_spec,
        out_shape=jax.ShapeDtypeStruct((n_rows, D_MODEL), F32),
        compiler_params=_cparams(("arbitrary",)),
    )(padded_end, dest[:, 0].reshape(nb, 1, tmb), dest[:, 1].reshape(nb, 1, tmb), x1)


def _expert_kernel(bexp_ref, nused_ref, x_ref, wg_ref, wu_ref, wd_ref, y_ref):
    del bexp_ref
    i = pl.program_id(0)

    @pl.when(i < nused_ref[0])
    def _():
        xb = x_ref[...].astype(BF16)
        hg = jnp.dot(xb, wg_ref[0], preferred_element_type=F32)
        hu = jnp.dot(xb, wu_ref[0], preferred_element_type=F32)
        hid = (hg * jax.nn.sigmoid(hg)) * hu
        y_ref[...] = jnp.dot(hid.astype(BF16), wd_ref[0], preferred_element_type=F32)

    @pl.when(i >= nused_ref[0])
    def _():
        y_ref[...] = jnp.zeros_like(y_ref)


def _experts(x_rows, block_expert, n_used, wg, wu, wd):
    n_rows = x_rows.shape[0]
    n_blocks = n_rows // MOE_ROWS
    grid_spec = pltpu.PrefetchScalarGridSpec(
        num_scalar_prefetch=2,
        grid=(n_blocks,),
        in_specs=[pl.BlockSpec((MOE_ROWS, D_MODEL), lambda i, be, nu: (jnp.minimum(i, nu[0] - 1), 0)),
                  pl.BlockSpec((1, D_MODEL, D_FF_EXPERT), lambda i, be, nu: (be[i], 0, 0)),
                  pl.BlockSpec((1, D_MODEL, D_FF_EXPERT), lambda i, be, nu: (be[i], 0, 0)),
                  pl.BlockSpec((1, D_FF_EXPERT, D_MODEL), lambda i, be, nu: (be[i], 0, 0))],
        out_specs=pl.BlockSpec((MOE_ROWS, D_MODEL), lambda i, be, nu: (i, 0)),
    )
    return pl.pallas_call(
        _expert_kernel,
        grid_spec=grid_spec,
        out_shape=jax.ShapeDtypeStruct((n_rows, D_MODEL), F32),
        compiler_params=_cparams(("arbitrary",)),
    )(block_expert, n_used, x_rows, wg, wu, wd)


def _combine_kernel(d0_ref, d1_ref, d0n_ref, d1n_ref, x_ref, route_ref, lng_ref, lnb_ref, y_hbm,
                    o_ref, ybuf, sems, *, alpha, tc, nb):
    i = pl.program_id(0)

    def row_copy(src, slot, choice, r):
        return pltpu.make_async_copy(y_hbm.at[pl.ds(src, 1), :],
                                     ybuf.at[slot, choice, pl.ds(r, 1), :], sems.at[slot])

    def fetch(da_ref, db_ref, slot):
        def issue(g, c):
            for u in range(DMA_UNROLL):
                r = g * DMA_UNROLL + u
                row_copy(da_ref[0, 0, r], slot, 0, r).start()
                row_copy(db_ref[0, 0, r], slot, 1, r).start()
            return c

        lax.fori_loop(0, tc // DMA_UNROLL, issue, 0)

    @pl.when(i == 0)
    def _():
        fetch(d0_ref, d1_ref, 0)

    for slot in range(2):
        @pl.when((i + 1 < nb) & ((i + 1) % 2 == slot))
        def _():
            fetch(d0n_ref, d1n_ref, slot)

    slot = i % 2

    for choice in range(TOP_K):
        pltpu.make_async_copy(y_hbm.at[pl.ds(0, tc), :], ybuf.at[slot, choice], sems.at[slot]).wait()

    r = route_ref[...]
    ffn = r[:, 2:3] * ybuf[slot, 0] + r[:, 3:4] * ybuf[slot, 1]
    o_ref[...] = _layer_norm(alpha * x_ref[...] + ffn, lng_ref[...], lnb_ref[...])


def _combine(x1, y_rows, dest, route, lng, lnb, alpha, tc):
    n = x1.shape[0]
    nb = n // tc
    const2 = lambda i: (0, 0)
    cur = pl.BlockSpec((1, 1, tc), lambda i: (i, 0, 0), memory_space=pltpu.SMEM)
    nxt = pl.BlockSpec((1, 1, tc), lambda i: (jnp.minimum(i + 1, nb - 1), 0, 0), memory_space=pltpu.SMEM)
    d0 = dest[:, 0].reshape(nb, 1, tc)
    d1 = dest[:, 1].reshape(nb, 1, tc)
    return pl.pallas_call(
        functools.partial(_combine_kernel, alpha=alpha, tc=tc, nb=nb),
        grid=(nb,),
        in_specs=[cur, cur, nxt, nxt,
                  pl.BlockSpec((tc, D_MODEL), lambda i: (i, 0)),
                  pl.BlockSpec((tc, LANES), lambda i: (i, 0)),
                  pl.BlockSpec((1, D_MODEL), const2),
                  pl.BlockSpec((1, D_MODEL), const2),
                  pl.BlockSpec(memory_space=pl.ANY)],
        out_specs=pl.BlockSpec((tc, D_MODEL), lambda i: (i, 0)),
        out_shape=jax.ShapeDtypeStruct((n, D_MODEL), F32),
        scratch_shapes=[pltpu.VMEM((2, TOP_K, tc, D_MODEL), F32),
                        pltpu.SemaphoreType.DMA((2,))],
        compiler_params=_cparams(("arbitrary",)),
    )(d0, d1, d0, d1, x1, route, lng, lnb, y_rows)


def _take_cols_kernel(src_ref, a_ref, b_ref, o_ref, *, shift):
    del src_ref
    if shift == 0:
        o_ref[...] = a_ref[...].astype(o_ref.dtype)
    else:
        lane = lax.broadcasted_iota(jnp.int32, a_ref.shape, 1)
        o_ref[...] = jnp.where(lane < LANES - shift,
                               pltpu.roll(a_ref[...], LANES - shift, axis=1),
                               pltpu.roll(b_ref[...], LANES - shift, axis=1)).astype(o_ref.dtype)


def _take_cols(w_all, layer, src_tiles, shift):
    d = w_all.shape[1]
    last = (w_all.shape[2] - 1) // LANES
    src = jnp.asarray(np.asarray(src_tiles, np.int32))
    grid_spec = pltpu.PrefetchScalarGridSpec(
        num_scalar_prefetch=1,
        grid=(len(src_tiles),),
        in_specs=[pl.BlockSpec((None, d, LANES), lambda j, s: (layer, 0, s[j])),
                  pl.BlockSpec((None, d, LANES), lambda j, s: (layer, 0, jnp.minimum(s[j] + 1, last)))],
        out_specs=pl.BlockSpec((d, LANES), lambda j, s: (0, j)),
    )
    return pl.pallas_call(
        functools.partial(_take_cols_kernel, shift=shift),
        grid_spec=grid_spec,
        out_shape=jax.ShapeDtypeStruct((d, len(src_tiles) * LANES), BF16),
        compiler_params=_cparams(("arbitrary",)),
    )(src, w_all, w_all)


def _permute_w_in(w_all, layer):
    per = MIX_WIDTH // LANES
    tiles = lambda t0, nt: list(range(t0, t0 + nt))
    pre = 7 * per
    aligned = tiles(0, per) + tiles(2 * per, 2 * per) + tiles(4 * per, 3 * per)
    gates = tiles(pre + 4 * per, 3 * D_MODEL // LANES)
    c_part = tiles(pre, 4 * per)
    main = jnp.concatenate([_take_cols(w_all, layer, gates, N_HEADS),
                            _take_cols(w_all, layer, aligned, 0),
                            _take_cols(w_all, layer, c_part, N_HEADS)], axis=1)
    fpart = _take_cols(w_all, layer, tiles(per, per) + [pre], 0)
    return main, fpart


def kernel(x, w_in, w_branch, w_out, fox_fgate_bias, hgrn_lb_logits, ln1_g, ln1_b,
           w_router_group, w_router_expert, w_up, w_gate, w_down, ln2_g, ln2_b):
    bsz, seq, d = x.shape
    depth = w_in.shape[0]
    n = bsz * seq
    alpha = float((2 * depth) ** 0.25)
    tm = min(1024, n)
    tb = min(512, seq)
    fox_sub = 2 if seq % (2 * FOX_BLOCK) == 0 else 1

    lb_cum = jnp.cumsum(jax.nn.softmax(hgrn_lb_logits.astype(F32), axis=0), axis=0)
    lower_bounds = lb_cum - lb_cum[0]
    tables = _retention_tables(seq)

    h = x.reshape(n, d)
    for layer in range(depth):
        w_main, w_f = _permute_w_in(w_in, layer)
        pb = _project(h, w_main, BF16, tm, 1024)
        pf = _project(h, w_f, F32, tm, N_FGATE)

        lb = lower_bounds[layer][None, :]
        ya = _hgrn2(pb, pf, jnp.log(lb), jnp.log1p(-lb), bsz, seq, tb)

        bias_col = jnp.concatenate([fox_fgate_bias[layer].astype(F32),
                                    jnp.zeros((SUBLANES - N_HEADS,), F32)])[:, None]
        neg_cum = _fox_neg_cum(pf, bias_col, bsz, seq)
        yb = _fox(pb, neg_cum, bsz, seq, fox_sub)

        yc = _retention(pb, tables, bsz, seq, tb)

        w_route = jnp.concatenate(
            [w_router_group[layer], w_router_expert[layer],
             jnp.zeros((d, LANES - N_GROUPS - N_EXPERTS), F32)], axis=1).astype(F32)
        wr_hi = w_route.astype(BF16)
        wr_lo = (w_route - wr_hi.astype(F32)).astype(BF16)
        x1, route = _merge(h, ya, yb, yc, pb, w_branch[layer].astype(BF16), w_out[layer].astype(BF16),
                           ln1_g[layer][None, :], ln1_b[layer][None, :], wr_hi, wr_lo, alpha, min(512, n))

        rank, counts = _rank(route, min(1024, n))
        dest, block_expert, n_used, padded_end, n_rows = _dispatch_plan(route, rank, counts, n)
        x_rows = _dispatch(x1, dest, padded_end, n_rows, min(2048, n))
        y_rows = _experts(x_rows, block_expert, n_used, w_gate[layer].astype(BF16),
                          w_up[layer].astype(BF16), w_down[layer].astype(BF16))
        h = _combine(x1, y_rows, dest, route, ln2_g[layer][None, :], ln2_b[layer][None, :], alpha,
                     min(512, n))
    return h.reshape(bsz, seq, d)
```

```python
import functools
import math

import numpy as np
import jax
import jax.numpy as jnp
from jax import lax
from jax.experimental import pallas as pl
from jax.experimental.pallas import tpu as pltpu

D_MODEL = 1024
HEAD_DIM = 128
MIX_WIDTH = D_MODEL // 2
N_HEADS = MIX_WIDTH // HEAD_DIM
N_GROUPS = 4
EXPERTS_PER_GROUP = 8
N_EXPERTS = N_GROUPS * EXPERTS_PER_GROUP
TOP_K = 2
D_FF_EXPERT = D_MODEL // 2
LN_EPS = 1e-5
HEAD_NORM_EPS = 1e-6
RET_ROPE_BASE = 10000.0

LANES = 128
SUBLANES = 8
HGRN_SUB = 2 * SUBLANES
HGRN_UNROLL = 2
RET_CHUNK = 128
FOX_BLOCK = 512
MOE_ROWS = 512
DMA_UNROLL = 8
VMEM_LIMIT = 48 * 1024 * 1024
LOG2E = 1.4426950408889634

COL_GA, COL_GB, COL_GC = 0, 1024, 2048
COL_AQ, COL_AI, COL_AG = 3072, 3584, 4096
COL_BQ, COL_BK, COL_BV = 4608, 5120, 5632
COL_CQ, COL_CK, COL_CV, COL_CG = 6144, 6656, 7168, 7680
N_MAIN = 8192
N_FGATE = MIX_WIDTH + LANES

F32 = jnp.float32
BF16 = jnp.bfloat16


def _cparams(sem):
    return pltpu.CompilerParams(dimension_semantics=sem, vmem_limit_bytes=VMEM_LIMIT)


def _proj_kernel(x_ref, w_ref, o_ref):
    o_ref[...] = jnp.dot(x_ref[...].astype(BF16), w_ref[...],
                         preferred_element_type=F32).astype(o_ref.dtype)


def _project(x2d, w, out_dtype, tm, tn):
    n, d = x2d.shape
    c = w.shape[1]
    return pl.pallas_call(
        _proj_kernel,
        grid=(n // tm, c // tn),
        in_specs=[pl.BlockSpec((tm, d), lambda i, j: (i, 0)),
                  pl.BlockSpec((d, tn), lambda i, j: (0, j))],
        out_specs=pl.BlockSpec((tm, tn), lambda i, j: (i, j)),
        out_shape=jax.ShapeDtypeStruct((n, c), out_dtype),
        compiler_params=_cparams(("parallel", "arbitrary")),
    )(x2d, w)


def _log_sigmoid(z):
    return jnp.minimum(z, 0.0) - jnp.log1p(jnp.exp(-jnp.abs(z)))


def _hgrn2_kernel(q_ref, i_ref, g_ref, f_ref, loglb_ref, log1m_ref, o_ref, st_ref, *, tb):
    @pl.when(pl.program_id(1) == 0)
    def _():
        st_ref[...] = jnp.zeros_like(st_ref)

    row = lax.broadcasted_iota(jnp.int32, (HGRN_SUB, HEAD_DIM), 0)
    row8 = lax.broadcasted_iota(jnp.int32, (SUBLANES, HEAD_DIM), 0)
    ninf = -jnp.inf

    def head_step(rows, h):
        cols = slice(h * HEAD_DIM, (h + 1) * HEAD_DIM)
        z = f_ref[rows, cols]
        a = loglb_ref[:, cols]
        b = log1m_ref[:, cols] + _log_sigmoid(z)
        log_f = jnp.maximum(a, b) + jnp.log1p(jnp.exp(-jnp.abs(a - b)))
        k = 1.0 - jnp.exp(log_f)
        q = q_ref[rows, cols].astype(F32)
        v = i_ref[rows, cols].astype(F32)
        cum = log_f * LOG2E
        for s in (1, 2, 4, 8):
            cum = cum + jnp.where(row >= s, pltpu.roll(cum, s, axis=0), 0.0)
        q_t, q_b = q[:SUBLANES], q[SUBLANES:]
        c_t, c_b = cum[:SUBLANES], cum[SUBLANES:]
        o_t = jnp.zeros((SUBLANES, HEAD_DIM), F32)
        o_b = jnp.zeros((SUBLANES, HEAD_DIM), F32)
        for s in range(HGRN_SUB):
            cs, ks, vs = cum[s:s + 1, :], k[s:s + 1, :], v[s:s + 1, :]
            if s < SUBLANES:
                w_b = q_b * ks * jnp.exp2(c_b - cs)
                e_t = c_t - cs
                if s > 0:
                    e_t = jnp.where(row8 >= s, e_t, ninf)
                w_t = q_t * ks * jnp.exp2(e_t)
                o_t = o_t + jnp.sum(w_t, axis=1, keepdims=True) * vs
            else:
                e_b = c_b - cs
                if s > SUBLANES:
                    e_b = jnp.where(row8 >= s - SUBLANES, e_b, ninf)
                w_b = q_b * ks * jnp.exp2(e_b)
            o_b = o_b + jnp.sum(w_b, axis=1, keepdims=True) * vs
        o = jnp.concatenate([o_t, o_b], axis=0)
        st = st_ref[h]
        qd = (q * jnp.exp2(cum)).astype(BF16)
        o = o + lax.dot_general(qd, st.astype(BF16), (((1,), (1,)), ((), ())),
                                preferred_element_type=F32)
        last = cum[HGRN_SUB - 1:HGRN_SUB, :]
        kd = (k * jnp.exp2(last - cum)).astype(BF16)
        upd = lax.dot_general(v.astype(BF16), kd, (((0,), (0,)), ((), ())),
                              preferred_element_type=F32)
        st_ref[h] = st * jnp.exp2(last) + upd
        y = o * lax.rsqrt(jnp.mean(o * o, axis=1, keepdims=True) + HEAD_NORM_EPS)
        y = y * jax.nn.sigmoid(g_ref[rows, cols].astype(F32))
        o_ref[rows, cols] = y.astype(o_ref.dtype)

    def step(j, carry):
        for u in range(HGRN_UNROLL):
            r0 = pl.multiple_of((j * HGRN_UNROLL + u) * HGRN_SUB, HGRN_SUB)
            for h in range(N_HEADS):
                head_step(pl.ds(r0, HGRN_SUB), h)
        return carry

    lax.fori_loop(0, tb // (HGRN_SUB * HGRN_UNROLL), step, 0)


def _hgrn2(pb, pf, log_lb, log1m_lb, bsz, seq, tb):
    n = bsz * seq
    nt = seq // tb
    cb = lambda c: c // MIX_WIDTH
    row_map = lambda c: (lambda b, t: (b * nt + t, c))
    return pl.pallas_call(
        functools.partial(_hgrn2_kernel, tb=tb),
        grid=(bsz, nt),
        in_specs=[pl.BlockSpec((tb, MIX_WIDTH), row_map(cb(COL_AQ))),
                  pl.BlockSpec((tb, MIX_WIDTH), row_map(cb(COL_AI))),
                  pl.BlockSpec((tb, MIX_WIDTH), row_map(cb(COL_AG))),
                  pl.BlockSpec((tb, MIX_WIDTH), row_map(0)),
                  pl.BlockSpec((1, MIX_WIDTH), lambda b, t: (0, 0)),
                  pl.BlockSpec((1, MIX_WIDTH), lambda b, t: (0, 0))],
        out_specs=pl.BlockSpec((tb, MIX_WIDTH), row_map(0)),
        out_shape=jax.ShapeDtypeStruct((n, MIX_WIDTH), BF16),
        scratch_shapes=[pltpu.VMEM((N_HEADS, HEAD_DIM, HEAD_DIM), F32)],
        compiler_params=_cparams(("parallel", "arbitrary")),
    )(pb, pb, pb, pf, log_lb, log1m_lb)


def _retention_kernel(q_ref, k_ref, v_ref, g_ref, cos_ref, sin_ref, dmask_ref, qdec_ref, kdec_ref,
                      o_ref, st_ref, *, tb, chunk_decay):
    @pl.when(pl.program_id(1) == 0)
    def _():
        st_ref[...] = jnp.zeros_like(st_ref)

    def step(j, carry):
        r0 = pl.multiple_of(j * RET_CHUNK, RET_CHUNK)
        rows = pl.ds(r0, RET_CHUNK)
        cos = cos_ref[rows, :]
        sin = sin_ref[rows, :]
        for h in range(N_HEADS):
            cols = slice(h * HEAD_DIM, (h + 1) * HEAD_DIM)
            q = q_ref[rows, cols].astype(F32)
            k = k_ref[rows, cols].astype(F32)
            q = q * cos + pltpu.roll(q, HEAD_DIM // 2, axis=1) * sin
            k = (k * cos + pltpu.roll(k, HEAD_DIM // 2, axis=1) * sin) * (HEAD_DIM ** -0.5)
            v = v_ref[rows, cols]
            inner = lax.dot_general(q.astype(BF16), k.astype(BF16), (((1,), (1,)), ((), ())),
                                    preferred_element_type=F32) * dmask_ref[h]
            st = st_ref[h]
            o = (jnp.dot(inner.astype(BF16), v, preferred_element_type=F32)
                 + jnp.dot((q * qdec_ref[h]).astype(BF16), st.astype(BF16),
                           preferred_element_type=F32))
            upd = lax.dot_general((k * kdec_ref[h]).astype(BF16), v, (((0,), (0,)), ((), ())),
                                  preferred_element_type=F32)
            st_ref[h] = st * chunk_decay[h] + upd
            c = o - jnp.mean(o, axis=1, keepdims=True)
            y = c * lax.rsqrt(jnp.mean(c * c, axis=1, keepdims=True) + HEAD_NORM_EPS)
            g = g_ref[rows, cols].astype(F32)
            o_ref[rows, cols] = (y * (g * jax.nn.sigmoid(g))).astype(o_ref.dtype)
        return carry

    lax.fori_loop(0, tb // RET_CHUNK, step, 0)


def _retention_tables(seq):
    half = HEAD_DIM // 2
    inv = 1.0 / (RET_ROPE_BASE ** jnp.linspace(0.0, 1.0, half, dtype=F32))
    ang = jnp.arange(seq, dtype=F32)[:, None] * inv[None, :]
    cos = jnp.cos(ang)
    sin = jnp.sin(ang)
    cos_t = jnp.concatenate([cos, cos], axis=-1)
    sin_t = jnp.concatenate([-sin, sin], axis=-1)
    log_gamma = jnp.log(1.0 - jnp.power(2.0, -5.0 - jnp.arange(N_HEADS, dtype=F32)))
    idx = jnp.arange(RET_CHUNK, dtype=F32)
    rel = idx[:, None] - idx[None, :]
    dmask = jnp.where(rel >= 0, jnp.exp(log_gamma[:, None, None] * jnp.maximum(rel, 0.0)), 0.0)
    ones = jnp.ones((1, 1, HEAD_DIM), F32)
    qdec = jnp.exp(log_gamma[:, None] * (idx + 1.0))[..., None] * ones
    kdec = jnp.exp(log_gamma[:, None] * (RET_CHUNK - 1.0 - idx))[..., None] * ones
    return cos_t, sin_t, dmask, qdec, kdec


def _retention(pb, tables, bsz, seq, tb):
    n = bsz * seq
    nt = seq // tb
    cos_t, sin_t, dmask, qdec, kdec = tables
    chunk_decay = tuple(float((1.0 - 2.0 ** (-5.0 - h)) ** RET_CHUNK) for h in range(N_HEADS))
    cb = lambda c: c // MIX_WIDTH
    row_map = lambda c: (lambda b, t: (b * nt + t, c))
    const3 = lambda b, t: (0, 0, 0)
    return pl.pallas_call(
        functools.partial(_retention_kernel, tb=tb, chunk_decay=chunk_decay),
        grid=(bsz, nt),
        in_specs=[pl.BlockSpec((tb, MIX_WIDTH), row_map(cb(COL_CQ))),
                  pl.BlockSpec((tb, MIX_WIDTH), row_map(cb(COL_CK))),
                  pl.BlockSpec((tb, MIX_WIDTH), row_map(cb(COL_CV))),
                  pl.BlockSpec((tb, MIX_WIDTH), row_map(cb(COL_CG))),
                  pl.BlockSpec((tb, HEAD_DIM), lambda b, t: (t, 0)),
                  pl.BlockSpec((tb, HEAD_DIM), lambda b, t: (t, 0)),
                  pl.BlockSpec((N_HEADS, RET_CHUNK, RET_CHUNK), const3),
                  pl.BlockSpec((N_HEADS, RET_CHUNK, HEAD_DIM), const3),
                  pl.BlockSpec((N_HEADS, RET_CHUNK, HEAD_DIM), const3)],
        out_specs=pl.BlockSpec((tb, MIX_WIDTH), row_map(0)),
        out_shape=jax.ShapeDtypeStruct((n, MIX_WIDTH), BF16),
        scratch_shapes=[pltpu.VMEM((N_HEADS, HEAD_DIM, HEAD_DIM), F32)],
        compiler_params=_cparams(("parallel", "arbitrary")),
    )(pb, pb, pb, pb, cos_t, sin_t, dmask, qdec, kdec)


def _fox_cum_kernel(z_ref, bias_ref, o_ref):
    z = z_ref[...]
    hi = z.astype(BF16)
    rem = z - hi.astype(F32)
    mid = rem.astype(BF16)
    lo = (rem - mid.astype(F32)).astype(BF16)
    sel = jnp.where(lax.broadcasted_iota(jnp.int32, (SUBLANES, LANES), 0)
                    == lax.broadcasted_iota(jnp.int32, (SUBLANES, LANES), 1), 1.0, 0.0).astype(BF16)
    nt = (((1,), (1,)), ((), ()))
    zt = (lax.dot_general(sel, hi, nt, preferred_element_type=F32)
          + lax.dot_general(sel, mid, nt, preferred_element_type=F32)
          + lax.dot_general(sel, lo, nt, preferred_element_type=F32))
    x = _log_sigmoid(zt + bias_ref[...])
    lane = lax.broadcasted_iota(jnp.int32, x.shape, 1)
    s = 1
    while s < x.shape[1]:
        x = x + jnp.where(lane >= s, pltpu.roll(x, s, axis=1), 0.0)
        s *= 2
    o_ref[...] = x * (-LOG2E)


def _fox_neg_cum(pf, bias_col, bsz, seq):
    return pl.pallas_call(
        _fox_cum_kernel,
        grid=(bsz,),
        in_specs=[pl.BlockSpec((seq, LANES), lambda b: (b, MIX_WIDTH // LANES)),
                  pl.BlockSpec((SUBLANES, 1), lambda b: (0, 0))],
        out_specs=pl.BlockSpec((None, SUBLANES, seq), lambda b: (b, 0, 0)),
        out_shape=jax.ShapeDtypeStruct((bsz, SUBLANES, seq), F32),
        compiler_params=_cparams(("parallel",)),
    )(pf, bias_col)


def _fox_kernel(q_ref, k_ref, v_ref, nck_ref, o_ref, m_ref, acc_ref, *, nsub, scale):
    i = pl.program_id(2)
    tk = FOX_BLOCK
    m_ref[...] = jnp.full_like(m_ref, -jnp.inf)
    acc_ref[...] = jnp.zeros_like(acc_ref)
    qs = [(q_ref[a * tk:(a + 1) * tk, :].astype(F32) * (scale * LOG2E)).astype(BF16)
          for a in range(nsub)]
    ones = jnp.ones((tk, HEAD_DIM), BF16)
    causal = (lax.broadcasted_iota(jnp.int32, (tk, tk), 1)
              <= lax.broadcasted_iota(jnp.int32, (tk, tk), 0))

    def attend(a, kblk, masked):
        rows = pl.ds(pl.multiple_of(kblk * tk, tk), tk)
        s = lax.dot_general(qs[a], k_ref[rows, :], (((1,), (1,)), ((), ())),
                            preferred_element_type=F32) + nck_ref[kblk]
        if masked:
            s = jnp.where(causal, s, -jnp.inf)
        m_prev = m_ref[a]
        m_new = jnp.maximum(m_prev, jnp.max(s, axis=1, keepdims=True))
        p = jnp.exp2(s - m_new).astype(BF16)
        v_aug = jnp.concatenate([v_ref[rows, :], ones], axis=1)
        acc_ref[a] = jnp.exp2(m_prev - m_new) * acc_ref[a] + jnp.dot(
            p, v_aug, preferred_element_type=F32)
        m_ref[a] = m_new

    def below_diagonal(j, c):
        for a in range(nsub):
            attend(a, j, False)
        return c

    lax.fori_loop(0, i * nsub, below_diagonal, 0)
    for d in range(nsub):
        for a in range(d, nsub):
            attend(a, i * nsub + d, a == d)
    for a in range(nsub):
        acc = acc_ref[a]
        o_ref[a * tk:(a + 1) * tk, :] = (acc[:, :HEAD_DIM] / acc[:, HEAD_DIM:HEAD_DIM + 1]).astype(o_ref.dtype)


def _fox(pb, neg_cum, bsz, seq, nsub):
    n = bsz * seq
    tq = nsub * FOX_BLOCK
    nq = seq // tq
    nkb = seq // FOX_BLOCK
    cq, ck, cv = COL_BQ // HEAD_DIM, COL_BK // HEAD_DIM, COL_BV // HEAD_DIM
    return pl.pallas_call(
        functools.partial(_fox_kernel, nsub=nsub, scale=HEAD_DIM ** -0.5),
        grid=(bsz, N_HEADS, nq),
        in_specs=[pl.BlockSpec((tq, HEAD_DIM), lambda b, h, i: (b * nq + i, cq + h)),
                  pl.BlockSpec((seq, HEAD_DIM), lambda b, h, i: (b, ck + h)),
                  pl.BlockSpec((seq, HEAD_DIM), lambda b, h, i: (b, cv + h)),
                  pl.BlockSpec((None, None, nkb, 1, FOX_BLOCK), lambda b, h, i: (b, h, 0, 0, 0))],
        out_specs=pl.BlockSpec((tq, HEAD_DIM), lambda b, h, i: (b * nq + i, h)),
        out_shape=jax.ShapeDtypeStruct((n, MIX_WIDTH), BF16),
        scratch_shapes=[pltpu.VMEM((nsub, FOX_BLOCK, 1), F32),
                        pltpu.VMEM((nsub, FOX_BLOCK, 2 * HEAD_DIM), F32)],
        compiler_params=_cparams(("parallel", "parallel", "arbitrary")),
    )(pb, pb, pb, neg_cum.reshape(bsz, SUBLANES, nkb, 1, FOX_BLOCK))


def _layer_norm(x, g, b):
    mu = jnp.mean(x, axis=1, keepdims=True)
    c = x - mu
    var = jnp.mean(c * c, axis=1, keepdims=True)
    return c * lax.rsqrt(var + LN_EPS) * g + b


def _route(logits):
    lane = lax.broadcasted_iota(jnp.int32, logits.shape, 1)
    lane_f = lane.astype(F32)
    ninf = -jnp.inf
    big = float(LANES)
    gl = jnp.where(lane < N_GROUPS, logits, ninf)
    gmax = jnp.max(gl, axis=1, keepdims=True)
    gidx = jnp.min(jnp.where(gl == gmax, lane_f, big), axis=1, keepdims=True)
    gprob = 1.0 / jnp.sum(jnp.exp(gl - gmax), axis=1, keepdims=True)
    e_group = ((lane - N_GROUPS) // EXPERTS_PER_GROUP).astype(F32)
    in_grp = (lane >= N_GROUPS) & (lane < N_GROUPS + N_EXPERTS) & (e_group == gidx)
    el = jnp.where(in_grp, logits, ninf)
    t1 = jnp.max(el, axis=1, keepdims=True)
    i1 = jnp.min(jnp.where(el == t1, lane_f, big), axis=1, keepdims=True)
    el2 = jnp.where(lane_f == i1, ninf, el)
    t2 = jnp.max(el2, axis=1, keepdims=True)
    i2 = jnp.min(jnp.where(el2 == t2, lane_f, big), axis=1, keepdims=True)
    d = jnp.exp(t2 - t1)
    g1 = gprob / (1.0 + d)
    g2 = gprob * d / (1.0 + d)
    out = jnp.where(lane == 0, i1 - N_GROUPS, 0.0)
    out = jnp.where(lane == 1, i2 - N_GROUPS, out)
    out = jnp.where(lane == 2, g1, out)
    out = jnp.where(lane == 3, g2, out)
    return out


def _merge_kernel(x_ref, ya_ref, yb_ref, yc_ref, ga_ref, gb_ref, gc_ref, wb_ref, wo_ref,
                  lng_ref, lnb_ref, wrh_ref, wrl_ref, x1_ref, route_ref, *, alpha):
    def branch(y_ref, g_ref, idx):
        return jax.nn.sigmoid(g_ref[...].astype(F32)) * jnp.dot(
            y_ref[...], wb_ref[idx], preferred_element_type=F32)

    merged = branch(ya_ref, ga_ref, 0) + branch(yb_ref, gb_ref, 1) + branch(yc_ref, gc_ref, 2)
    mix = jnp.dot(merged.astype(BF16), wo_ref[...], preferred_element_type=F32)
    x1 = _layer_norm(alpha * x_ref[...] + mix, lng_ref[...], lnb_ref[...])
    x1_ref[...] = x1
    x_hi = x1.astype(BF16)
    x_lo = (x1 - x_hi.astype(F32)).astype(BF16)
    logits = (jnp.dot(x_hi, wrh_ref[...], preferred_element_type=F32)
              + jnp.dot(x_lo, wrh_ref[...], preferred_element_type=F32)
              + jnp.dot(x_hi, wrl_ref[...], preferred_element_type=F32))
    route_ref[...] = _route(logits)


def _merge(x2d, ya, yb, yc, pb, wb, wo, lng, lnb, wr_hi, wr_lo, alpha, tm):
    n = x2d.shape[0]
    row = lambda c: (lambda i: (i, c))
    const2 = lambda i: (0, 0)
    return pl.pallas_call(
        functools.partial(_merge_kernel, alpha=alpha),
        grid=(n // tm,),
        in_specs=[pl.BlockSpec((tm, D_MODEL), row(0)),
                  pl.BlockSpec((tm, MIX_WIDTH), row(0)),
                  pl.BlockSpec((tm, MIX_WIDTH), row(0)),
                  pl.BlockSpec((tm, MIX_WIDTH), row(0)),
                  pl.BlockSpec((tm, D_MODEL), row(COL_GA // D_MODEL)),
                  pl.BlockSpec((tm, D_MODEL), row(COL_GB // D_MODEL)),
                  pl.BlockSpec((tm, D_MODEL), row(COL_GC // D_MODEL)),
                  pl.BlockSpec((3, MIX_WIDTH, D_MODEL), lambda i: (0, 0, 0)),
                  pl.BlockSpec((D_MODEL, D_MODEL), const2),
                  pl.BlockSpec((1, D_MODEL), const2),
                  pl.BlockSpec((1, D_MODEL), const2),
                  pl.BlockSpec((D_MODEL, LANES), const2),
                  pl.BlockSpec((D_MODEL, LANES), const2)],
        out_specs=[pl.BlockSpec((tm, D_MODEL), row(0)),
                   pl.BlockSpec((tm, LANES), row(0))],
        out_shape=[jax.ShapeDtypeStruct((n, D_MODEL), F32),
                   jax.ShapeDtypeStruct((n, LANES), F32)],
        compiler_params=_cparams(("parallel",)),
    )(x2d, ya, yb, yc, pb, pb, pb, wb, wo, lng, lnb, wr_hi, wr_lo)


def _rank_kernel(route_ref, rank_ref, cnt_ref, carry_ref, *, tr):
    @pl.when(pl.program_id(0) == 0)
    def _():
        carry_ref[...] = jnp.zeros_like(carry_ref)

    r = route_ref[...]
    lane = lax.broadcasted_iota(jnp.int32, r.shape, 1)
    lane_f = lane.astype(F32)
    oh1 = lane_f == r[:, 0:1]
    oh2 = lane_f == r[:, 1:2]
    oh = jnp.where(oh1, 1.0, jnp.where(oh2, 1.0, 0.0))
    earlier = jnp.where(lax.broadcasted_iota(jnp.int32, (tr, tr), 1)
                        < lax.broadcasted_iota(jnp.int32, (tr, tr), 0), 1.0, 0.0).astype(BF16)
    before = jnp.dot(earlier, oh.astype(BF16), preferred_element_type=F32) + carry_ref[...]
    rank1 = jnp.sum(jnp.where(oh1, before, 0.0), axis=1, keepdims=True)
    rank2 = jnp.sum(jnp.where(oh2, before, 0.0), axis=1, keepdims=True)
    rank_ref[...] = jnp.where(lane == 0, rank1, jnp.where(lane == 1, rank2, 0.0))
    total = carry_ref[...] + jnp.sum(oh, axis=0, keepdims=True)
    carry_ref[...] = total
    cnt_ref[...] = total


def _rank(route, tr):
    n = route.shape[0]
    return pl.pallas_call(
        functools.partial(_rank_kernel, tr=tr),
        grid=(n // tr,),
        in_specs=[pl.BlockSpec((tr, LANES), lambda i: (i, 0))],
        out_specs=[pl.BlockSpec((tr, LANES), lambda i: (i, 0)),
                   pl.BlockSpec((1, LANES), lambda i: (0, 0))],
        out_shape=[jax.ShapeDtypeStruct((n, LANES), F32),
                   jax.ShapeDtypeStruct((1, LANES), F32)],
        scratch_shapes=[pltpu.VMEM((1, LANES), F32)],
        compiler_params=_cparams(("arbitrary",)),
    )(route)


def _dispatch_plan(route, rank, counts, n_tok):
    n_rows = n_tok * TOP_K + N_EXPERTS * MOE_ROWS
    n_blocks = n_rows // MOE_ROWS
    cnt = counts[0, :N_EXPERTS].astype(jnp.int32)
    padded = (cnt + MOE_ROWS - 1) // MOE_ROWS * MOE_ROWS
    padded_end = jnp.cumsum(padded)
    padded_start = padded_end - padded
    expert = route[:, :TOP_K].astype(jnp.int32)
    start = jnp.sum(jnp.where(expert[:, :, None] == jnp.arange(N_EXPERTS)[None, None, :],
                              padded_start[None, None, :], 0), axis=-1)
    dest = start + rank[:, :TOP_K].astype(jnp.int32)
    block_start = jnp.arange(n_blocks, dtype=jnp.int32) * MOE_ROWS
    block_expert = jnp.minimum(jnp.sum(block_start[:, None] >= padded_end[None, :], axis=1),
                               N_EXPERTS - 1).astype(jnp.int32)
    n_used = (padded_end[-1:] // MOE_ROWS).astype(jnp.int32)
    return dest, block_expert, n_used, padded_end.astype(jnp.int32), n_rows


def _dispatch_kernel(pend_ref, d0_ref, d1_ref, x_ref, xr_hbm, zbuf, sem_z, sem, *, tmb, n_blocks):
    i = pl.program_id(0)

    def zero_copy(e):
        end = pend_ref[e]
        return pltpu.make_async_copy(
            zbuf, xr_hbm.at[pl.ds(pl.multiple_of(end - MOE_ROWS, MOE_ROWS), MOE_ROWS), :], sem_z)

    def nonempty(e):
        return pend_ref[e] > (pend_ref[e - 1] if e > 0 else 0)

    @pl.when(i == 0)
    def _():
        zbuf[...] = jnp.zeros_like(zbuf)
        for e in range(N_EXPERTS):
            @pl.when(nonempty(e))
            def _():
                zero_copy(e).start()
        for e in range(N_EXPERTS):
            @pl.when(nonempty(e))
            def _():
                zero_copy(e).wait()

        def tail_copy(b):
            return pltpu.make_async_copy(
                zbuf, xr_hbm.at[pl.ds(pl.multiple_of(b * MOE_ROWS, MOE_ROWS), MOE_ROWS), :], sem_z)

        def tail_start(b, c):
            tail_copy(b).start()
            return c

        def tail_wait(b, c):
            tail_copy(b).wait()
            return c

        first_unused = pend_ref[N_EXPERTS - 1] // MOE_ROWS
        lax.fori_loop(first_unused, n_blocks, tail_start, 0)
        lax.fori_loop(first_unused, n_blocks, tail_wait, 0)

    def row_copy(r, dst):
        return pltpu.make_async_copy(x_ref.at[pl.ds(r, 1), :], xr_hbm.at[pl.ds(dst, 1), :], sem)

    def issue(g, c):
        for u in range(DMA_UNROLL):
            r = g * DMA_UNROLL + u
            row_copy(r, d0_ref[0, 0, r]).start()
            row_copy(r, d1_ref[0, 0, r]).start()
        return c

    lax.fori_loop(0, tmb // DMA_UNROLL, issue, 0)

    for _ in range(TOP_K):
        pltpu.make_async_copy(x_ref, xr_hbm.at[pl.ds(0, tmb), :], sem).wait()


def _dispatch(x1, dest, padded_end, n_rows, tmb):
    n = x1.shape[0]
    nb = n // tmb
    idx_spec = pl.BlockSpec((1, 1, tmb), lambda i, pe: (i, 0, 0), memory_space=pltpu.SMEM)
    grid_spec = pltpu.PrefetchScalarGridSpec(
        num_scalar_prefetch=1,
        grid=(nb,),
        in_specs=[idx_spec, idx_spec, pl.BlockSpec((tmb, D_MODEL), lambda i, pe: (i, 0))],
        out_specs=pl.BlockSpec(memory_space=pl.ANY),
        scratch_shapes=[pltpu.VMEM((MOE_ROWS, D_MODEL), F32),
                        pltpu.SemaphoreType.DMA, pltpu.SemaphoreType.DMA],
    )
    return pl.pallas_call(
        functools.partial(_dispatch_kernel, tmb=tmb, n_blocks=n_rows // MOE_ROWS),
        grid_spec=grid_spec,
        out_shape=jax.ShapeDtypeStruct((n_rows, D_MODEL), F32),
        compiler_params=_cparams(("arbitrary",)),
    )(padded_end, dest[:, 0].reshape(nb, 1, tmb), dest[:, 1].reshape(nb, 1, tmb), x1)


def _expert_kernel(bexp_ref, nused_ref, x_ref, wg_ref, wu_ref, wd_ref, y_ref):
    del bexp_ref
    i = pl.program_id(0)

    @pl.when(i < nused_ref[0])
    def _():
        xb = x_ref[...].astype(BF16)
        hg = jnp.dot(xb, wg_ref[0], preferred_element_type=F32)
        hu = jnp.dot(xb, wu_ref[0], preferred_element_type=F32)
        hid = (hg * jax.nn.sigmoid(hg)) * hu
        y_ref[...] = jnp.dot(hid.astype(BF16), wd_ref[0], preferred_element_type=F32)

    @pl.when(i >= nused_ref[0])
    def _():
        y_ref[...] = jnp.zeros_like(y_ref)


def _experts(x_rows, block_expert, n_used, wg, wu, wd):
    n_rows = x_rows.shape[0]
    n_blocks = n_rows // MOE_ROWS
    grid_spec = pltpu.PrefetchScalarGridSpec(
        num_scalar_prefetch=2,
        grid=(n_blocks,),
        in_specs=[pl.BlockSpec((MOE_ROWS, D_MODEL), lambda i, be, nu: (jnp.minimum(i, nu[0] - 1), 0)),
                  pl.BlockSpec((1, D_MODEL, D_FF_EXPERT), lambda i, be, nu: (be[i], 0, 0)),
                  pl.BlockSpec((1, D_MODEL, D_FF_EXPERT), lambda i, be, nu: (be[i], 0, 0)),
                  pl.BlockSpec((1, D_FF_EXPERT, D_MODEL), lambda i, be, nu: (be[i], 0, 0))],
        out_specs=pl.BlockSpec((MOE_ROWS, D_MODEL), lambda i, be, nu: (i, 0)),
    )
    return pl.pallas_call(
        _expert_kernel,
        grid_spec=grid_spec,
        out_shape=jax.ShapeDtypeStruct((n_rows, D_MODEL), F32),
        compiler_params=_cparams(("arbitrary",)),
    )(block_expert, n_used, x_rows, wg, wu, wd)


def _combine_kernel(d0_ref, d1_ref, d0n_ref, d1n_ref, x_ref, route_ref, lng_ref, lnb_ref, y_hbm,
                    o_ref, ybuf, sems, *, alpha, tc, nb):
    i = pl.program_id(0)

    def row_copy(src, slot, choice, r):
        return pltpu.make_async_copy(y_hbm.at[pl.ds(src, 1), :],
                                     ybuf.at[slot, choice, pl.ds(r, 1), :], sems.at[slot])

    def fetch(da_ref, db_ref, slot):
        def issue(g, c):
            for u in range(DMA_UNROLL):
                r = g * DMA_UNROLL + u
                row_copy(da_ref[0, 0, r], slot, 0, r).start()
                row_copy(db_ref[0, 0, r], slot, 1, r).start()
            return c

        lax.fori_loop(0, tc // DMA_UNROLL, issue, 0)

    @pl.when(i == 0)
    def _():
        fetch(d0_ref, d1_ref, 0)

    for slot in range(2):
        @pl.when((i + 1 < nb) & ((i + 1) % 2 == slot))
        def _():
            fetch(d0n_ref, d1n_ref, slot)

    slot = i % 2

    for choice in range(TOP_K):
        pltpu.make_async_copy(y_hbm.at[pl.ds(0, tc), :], ybuf.at[slot, choice], sems.at[slot]).wait()

    r = route_ref[...]
    ffn = r[:, 2:3] * ybuf[slot, 0] + r[:, 3:4] * ybuf[slot, 1]
    o_ref[...] = _layer_norm(alpha * x_ref[...] + ffn, lng_ref[...], lnb_ref[...])


def _combine(x1, y_rows, dest, route, lng, lnb, alpha, tc):
    n = x1.shape[0]
    nb = n // tc
    const2 = lambda i: (0, 0)
    cur = pl.BlockSpec((1, 1, tc), lambda i: (i, 0, 0), memory_space=pltpu.SMEM)
    nxt = pl.BlockSpec((1, 1, tc), lambda i: (jnp.minimum(i + 1, nb - 1), 0, 0), memory_space=pltpu.SMEM)
    d0 = dest[:, 0].reshape(nb, 1, tc)
    d1 = dest[:, 1].reshape(nb, 1, tc)
    return pl.pallas_call(
        functools.partial(_combine_kernel, alpha=alpha, tc=tc, nb=nb),
        grid=(nb,),
        in_specs=[cur, cur, nxt, nxt,
                  pl.BlockSpec((tc, D_MODEL), lambda i: (i, 0)),
                  pl.BlockSpec((tc, LANES), lambda i: (i, 0)),
                  pl.BlockSpec((1, D_MODEL), const2),
                  pl.BlockSpec((1, D_MODEL), const2),
                  pl.BlockSpec(memory_space=pl.ANY)],
        out_specs=pl.BlockSpec((tc, D_MODEL), lambda i: (i, 0)),
        out_shape=jax.ShapeDtypeStruct((n, D_MODEL), F32),
        scratch_shapes=[pltpu.VMEM((2, TOP_K, tc, D_MODEL), F32),
                        pltpu.SemaphoreType.DMA((2,))],
        compiler_params=_cparams(("arbitrary",)),
    )(d0, d1, d0, d1, x1, route, lng, lnb, y_rows)


def _take_cols_kernel(src_ref, a_ref, b_ref, o_ref, *, shift):
    del src_ref
    if shift == 0:
        o_ref[...] = a_ref[...].astype(o_ref.dtype)
    else:
        lane = lax.broadcasted_iota(jnp.int32, a_ref.shape, 1)
        o_ref[...] = jnp.where(lane < LANES - shift,
                               pltpu.roll(a_ref[...], LANES - shift, axis=1),
                               pltpu.roll(b_ref[...], LANES - shift, axis=1)).astype(o_ref.dtype)


def _take_cols(w_all, layer, src_tiles, shift):
    d = w_all.shape[1]
    last = (w_all.shape[2] - 1) // LANES
    src = jnp.asarray(np.asarray(src_tiles, np.int32))
    grid_spec = pltpu.PrefetchScalarGridSpec(
        num_scalar_prefetch=1,
        grid=(len(src_tiles),),
        in_specs=[pl.BlockSpec((None, d, LANES), lambda j, s: (layer, 0, s[j])),
                  pl.BlockSpec((None, d, LANES), lambda j, s: (layer, 0, jnp.minimum(s[j] + 1, last)))],
        out_specs=pl.BlockSpec((d, LANES), lambda j, s: (0, j)),
    )
    return pl.pallas_call(
        functools.partial(_take_cols_kernel, shift=shift),
        grid_spec=grid_spec,
        out_shape=jax.ShapeDtypeStruct((d, len(src_tiles) * LANES), BF16),
        compiler_params=_cparams(("arbitrary",)),
    )(src, w_all, w_all)


def _permute_w_in(w_all, layer):
    per = MIX_WIDTH // LANES
    tiles = lambda t0, nt: list(range(t0, t0 + nt))
    pre = 7 * per
    aligned = tiles(0, per) + tiles(2 * per, 2 * per) + tiles(4 * per, 3 * per)
    gates = tiles(pre + 4 * per, 3 * D_MODEL // LANES)
    c_part = tiles(pre, 4 * per)
    main = jnp.concatenate([_take_cols(w_all, layer, gates, N_HEADS),
                            _take_cols(w_all, layer, aligned, 0),
                            _take_cols(w_all, layer, c_part, N_HEADS)], axis=1)
    fpart = _take_cols(w_all, layer, tiles(per, per) + [pre], 0)
    return main, fpart


def kernel(x, w_in, w_branch, w_out, fox_fgate_bias, hgrn_lb_logits, ln1_g, ln1_b,
           w_router_group, w_router_expert, w_up, w_gate, w_down, ln2_g, ln2_b):
    bsz, seq, d = x.shape
    depth = w_in.shape[0]
    n = bsz * seq
    alpha = float((2 * depth) ** 0.25)
    tm = min(1024, n)
    tb = min(512, seq)
    fox_sub = max(s for s in (1, 2, 4) if seq % (s * FOX_BLOCK) == 0)

    lb_cum = jnp.cumsum(jax.nn.softmax(hgrn_lb_logits.astype(F32), axis=0), axis=0)
    lower_bounds = lb_cum - lb_cum[0]
    tables = _retention_tables(seq)

    h = x.reshape(n, d)
    for layer in range(depth):
        w_main, w_f = _permute_w_in(w_in, layer)
        pb = _project(h, w_main, BF16, tm, 1024)
        pf = _project(h, w_f, F32, tm, N_FGATE)

        lb = lower_bounds[layer][None, :]
        ya = _hgrn2(pb, pf, jnp.log(lb), jnp.log1p(-lb), bsz, seq, tb)

        bias_col = jnp.concatenate([fox_fgate_bias[layer].astype(F32),
                                    jnp.zeros((SUBLANES - N_HEADS,), F32)])[:, None]
        neg_cum = _fox_neg_cum(pf, bias_col, bsz, seq)
        yb = _fox(pb, neg_cum, bsz, seq, fox_sub)

        yc = _retention(pb, tables, bsz, seq, tb)

        w_route = jnp.concatenate(
            [w_router_group[layer], w_router_expert[layer],
             jnp.zeros((d, LANES - N_GROUPS - N_EXPERTS), F32)], axis=1).astype(F32)
        wr_hi = w_route.astype(BF16)
        wr_lo = (w_route - wr_hi.astype(F32)).astype(BF16)
        x1, route = _merge(h, ya, yb, yc, pb, w_branch[layer].astype(BF16), w_out[layer].astype(BF16),
                           ln1_g[layer][None, :], ln1_b[layer][None, :], wr_hi, wr_lo, alpha, min(512, n))

        rank, counts = _rank(route, min(1024, n))
        dest, block_expert, n_used, padded_end, n_rows = _dispatch_plan(route, rank, counts, n)
        x_rows = _dispatch(x1, dest, padded_end, n_rows, min(2048, n))
        y_rows = _experts(x_rows, block_expert, n_used, w_gate[layer].astype(BF16),
                          w_up[layer].astype(BF16), w_down[layer].astype(BF16))
        h = _combine(x1, y_rows, dest, route, ln2_g[layer][None, :], ln2_b[layer][None, :], alpha,
                     min(512, n))
    return h.reshape(bsz, seq, d)
```

```python
import functools
import math

import numpy as np
import jax
import jax.numpy as jnp
from jax import lax
from jax.experimental import pallas as pl
from jax.experimental.pallas import tpu as pltpu

D_MODEL = 1024
HEAD_DIM = 128
MIX_WIDTH = D_MODEL // 2
N_HEADS = MIX_WIDTH // HEAD_DIM
N_GROUPS = 4
EXPERTS_PER_GROUP = 8
N_EXPERTS = N_GROUPS * EXPERTS_PER_GROUP
TOP_K = 2
D_FF_EXPERT = D_MODEL // 2
LN_EPS = 1e-5
HEAD_NORM_EPS = 1e-6
RET_ROPE_BASE = 10000.0

LANES = 128
SUBLANES = 8
HGRN_SUB = 2 * SUBLANES
HGRN_UNROLL = 2
RET_CHUNK = 256
FOX_BLOCK = 512
MOE_ROWS = 512
DMA_UNROLL = 8
VMEM_LIMIT = 48 * 1024 * 1024
LOG2E = 1.4426950408889634

COL_GA, COL_GB, COL_GC = 0, 1024, 2048
COL_AQ, COL_AI, COL_AG = 3072, 3584, 4096
COL_BQ, COL_BK, COL_BV = 4608, 5120, 5632
COL_CQ, COL_CK, COL_CV, COL_CG = 6144, 6656, 7168, 7680
N_MAIN = 8192
N_FGATE = MIX_WIDTH + LANES

F32 = jnp.float32
BF16 = jnp.bfloat16


def _cparams(sem):
    return pltpu.CompilerParams(dimension_semantics=sem, vmem_limit_bytes=VMEM_LIMIT)


def _proj_kernel(x_ref, w_ref, wf_ref, o_ref, of_ref, *, n_main):
    j = pl.program_id(1)
    xb = x_ref[...].astype(BF16)

    @pl.when(j < n_main)
    def _():
        o_ref[...] = jnp.dot(xb, w_ref[...], preferred_element_type=F32).astype(o_ref.dtype)

    @pl.when(j == n_main)
    def _():
        of_ref[...] = jnp.dot(xb, wf_ref[...], preferred_element_type=F32)


def _project(x2d, w, wf, tm, tn):
    n, d = x2d.shape
    c = w.shape[1]
    cf = wf.shape[1]
    n_main = c // tn
    last = n_main - 1
    return pl.pallas_call(
        functools.partial(_proj_kernel, n_main=n_main),
        grid=(n // tm, n_main + 1),
        in_specs=[pl.BlockSpec((tm, d), lambda i, j: (i, 0)),
                  pl.BlockSpec((d, tn), lambda i, j: (0, jnp.minimum(j, last))),
                  pl.BlockSpec((d, cf), lambda i, j: (0, 0))],
        out_specs=[pl.BlockSpec((tm, tn), lambda i, j: (i, jnp.minimum(j, last))),
                   pl.BlockSpec((tm, cf), lambda i, j: (i, 0))],
        out_shape=[jax.ShapeDtypeStruct((n, c), BF16),
                   jax.ShapeDtypeStruct((n, cf), F32)],
        compiler_params=_cparams(("parallel", "arbitrary")),
    )(x2d, w, wf)


def _log_sigmoid(z):
    return jnp.minimum(z, 0.0) - jnp.log1p(jnp.exp(-jnp.abs(z)))


def _hgrn2_kernel(q_ref, i_ref, g_ref, f_ref, loglb_ref, log1m_ref, o_ref, st_ref, rows_ref, *, tb):
    @pl.when(pl.program_id(1) == 0)
    def _():
        st_ref[...] = jnp.zeros_like(st_ref)

    row = lax.broadcasted_iota(jnp.int32, (HGRN_SUB, HEAD_DIM), 0)
    row8 = lax.broadcasted_iota(jnp.int32, (SUBLANES, HEAD_DIM), 0)
    ninf = -jnp.inf

    def head_step(rows, h, u):
        cols = slice(h * HEAD_DIM, (h + 1) * HEAD_DIM)
        z = f_ref[rows, cols]
        a = loglb_ref[:, cols]
        b = log1m_ref[:, cols] + _log_sigmoid(z)
        log_f = jnp.maximum(a, b) + jnp.log1p(jnp.exp(-jnp.abs(a - b)))
        k = 1.0 - jnp.exp(log_f)
        q = q_ref[rows, cols].astype(F32)
        v = i_ref[rows, cols].astype(F32)
        cum = log_f * LOG2E
        for s in (1, 2, 4, 8):
            cum = cum + jnp.where(row >= s, pltpu.roll(cum, s, axis=0), 0.0)
        q_t, q_b = q[:SUBLANES], q[SUBLANES:]
        c_t, c_b = cum[:SUBLANES], cum[SUBLANES:]
        o_t = jnp.zeros((SUBLANES, HEAD_DIM), F32)
        o_b = jnp.zeros((SUBLANES, HEAD_DIM), F32)
        rows_ref[u, h, 0] = cum
        rows_ref[u, h, 1] = k
        rows_ref[u, h, 2] = v
        for s in range(HGRN_SUB):
            cs, ks, vs = (rows_ref[u, h, t, s:s + 1, :] for t in range(3))
            if s < SUBLANES:
                w_b = q_b * ks * jnp.exp2(c_b - cs)
                e_t = c_t - cs
                if s > 0:
                    e_t = jnp.where(row8 >= s, e_t, ninf)
                w_t = q_t * ks * jnp.exp2(e_t)
                o_t = o_t + jnp.sum(w_t, axis=1, keepdims=True) * vs
            else:
                e_b = c_b - cs
                if s > SUBLANES:
                    e_b = jnp.where(row8 >= s - SUBLANES, e_b, ninf)
                w_b = q_b * ks * jnp.exp2(e_b)
            o_b = o_b + jnp.sum(w_b, axis=1, keepdims=True) * vs
        o = jnp.concatenate([o_t, o_b], axis=0)
        st = st_ref[h]
        qd = (q * jnp.exp2(cum)).astype(BF16)
        o = o + lax.dot_general(qd, st.astype(BF16), (((1,), (1,)), ((), ())),
                                preferred_element_type=F32)
        last = cum[HGRN_SUB - 1:HGRN_SUB, :]
        kd = (k * jnp.exp2(last - cum)).astype(BF16)
        upd = lax.dot_general(v.astype(BF16), kd, (((0,), (0,)), ((), ())),
                              preferred_element_type=F32)
        st_ref[h] = st * jnp.exp2(last) + upd
        y = o * lax.rsqrt(jnp.mean(o * o, axis=1, keepdims=True) + HEAD_NORM_EPS)
        y = y * jax.nn.sigmoid(g_ref[rows, cols].astype(F32))
        o_ref[rows, cols] = y.astype(o_ref.dtype)

    def step(j, carry):
        for u in range(HGRN_UNROLL):
            r0 = pl.multiple_of((j * HGRN_UNROLL + u) * HGRN_SUB, HGRN_SUB)
            for h in range(N_HEADS):
                head_step(pl.ds(r0, HGRN_SUB), h, u)
        return carry

    lax.fori_loop(0, tb // (HGRN_SUB * HGRN_UNROLL), step, 0)


def _hgrn2(pb, pf, log_lb, log1m_lb, bsz, seq, tb):
    n = bsz * seq
    nt = seq // tb
    cb = lambda c: c // MIX_WIDTH
    row_map = lambda c: (lambda b, t: (b * nt + t, c))
    return pl.pallas_call(
        functools.partial(_hgrn2_kernel, tb=tb),
        grid=(bsz, nt),
        in_specs=[pl.BlockSpec((tb, MIX_WIDTH), row_map(cb(COL_AQ))),
                  pl.BlockSpec((tb, MIX_WIDTH), row_map(cb(COL_AI))),
                  pl.BlockSpec((tb, MIX_WIDTH), row_map(cb(COL_AG))),
                  pl.BlockSpec((tb, MIX_WIDTH), row_map(0)),
                  pl.BlockSpec((1, MIX_WIDTH), lambda b, t: (0, 0)),
                  pl.BlockSpec((1, MIX_WIDTH), lambda b, t: (0, 0))],
        out_specs=pl.BlockSpec((tb, MIX_WIDTH), row_map(0)),
        out_shape=jax.ShapeDtypeStruct((n, MIX_WIDTH), BF16),
        scratch_shapes=[pltpu.VMEM((N_HEADS, HEAD_DIM, HEAD_DIM), F32),
                        pltpu.VMEM((HGRN_UNROLL, N_HEADS, 3, HGRN_SUB, HEAD_DIM), F32)],
        compiler_params=_cparams(("parallel", "arbitrary")),
    )(pb, pb, pb, pf, log_lb, log1m_lb)


def _retention_kernel(q_ref, k_ref, v_ref, g_ref, cos_ref, sin_ref, dmask_ref, qdec_ref, kdec_ref,
                      o_ref, st_ref, *, tb, chunk_decay):
    @pl.when(pl.program_id(1) == 0)
    def _():
        st_ref[...] = jnp.zeros_like(st_ref)

    def step(j, carry):
        r0 = pl.multiple_of(j * RET_CHUNK, RET_CHUNK)
        rows = pl.ds(r0, RET_CHUNK)
        cos = cos_ref[rows, :]
        sin = sin_ref[rows, :]
        for h in range(N_HEADS):
            cols = slice(h * HEAD_DIM, (h + 1) * HEAD_DIM)
            q = q_ref[rows, cols].astype(F32)
            k = k_ref[rows, cols].astype(F32)
            q = q * cos + pltpu.roll(q, HEAD_DIM // 2, axis=1) * sin
            k = (k * cos + pltpu.roll(k, HEAD_DIM // 2, axis=1) * sin) * (HEAD_DIM ** -0.5)
            v = v_ref[rows, cols]
            inner = lax.dot_general(q.astype(BF16), k.astype(BF16), (((1,), (1,)), ((), ())),
                                    preferred_element_type=F32) * dmask_ref[h]
            st = st_ref[h]
            o = (jnp.dot(inner.astype(BF16), v, preferred_element_type=F32)
                 + jnp.dot((q * qdec_ref[h]).astype(BF16), st.astype(BF16),
                           preferred_element_type=F32))
            upd = lax.dot_general((k * kdec_ref[h]).astype(BF16), v, (((0,), (0,)), ((), ())),
                                  preferred_element_type=F32)
            st_ref[h] = st * chunk_decay[h] + upd
            c = o - jnp.mean(o, axis=1, keepdims=True)
            y = c * lax.rsqrt(jnp.mean(c * c, axis=1, keepdims=True) + HEAD_NORM_EPS)
            g = g_ref[rows, cols].astype(F32)
            o_ref[rows, cols] = (y * (g * jax.nn.sigmoid(g))).astype(o_ref.dtype)
        return carry

    lax.fori_loop(0, tb // RET_CHUNK, step, 0)


def _retention_tables(seq):
    half = HEAD_DIM // 2
    inv = 1.0 / (RET_ROPE_BASE ** jnp.linspace(0.0, 1.0, half, dtype=F32))
    ang = jnp.arange(seq, dtype=F32)[:, None] * inv[None, :]
    cos = jnp.cos(ang)
    sin = jnp.sin(ang)
    cos_t = jnp.concatenate([cos, cos], axis=-1)
    sin_t = jnp.concatenate([-sin, sin], axis=-1)
    log_gamma = jnp.log(1.0 - jnp.power(2.0, -5.0 - jnp.arange(N_HEADS, dtype=F32)))
    idx = jnp.arange(RET_CHUNK, dtype=F32)
    rel = idx[:, None] - idx[None, :]
    dmask = jnp.where(rel >= 0, jnp.exp(log_gamma[:, None, None] * jnp.maximum(rel, 0.0)), 0.0)
    ones = jnp.ones((1, 1, HEAD_DIM), F32)
    qdec = jnp.exp(log_gamma[:, None] * (idx + 1.0))[..., None] * ones
    kdec = jnp.exp(log_gamma[:, None] * (RET_CHUNK - 1.0 - idx))[..., None] * ones
    return cos_t, sin_t, dmask, qdec, kdec


def _retention(pb, tables, bsz, seq, tb):
    n = bsz * seq
    nt = seq // tb
    cos_t, sin_t, dmask, qdec, kdec = tables
    chunk_decay = tuple(float((1.0 - 2.0 ** (-5.0 - h)) ** RET_CHUNK) for h in range(N_HEADS))
    cb = lambda c: c // MIX_WIDTH
    row_map = lambda c: (lambda b, t: (b * nt + t, c))
    const3 = lambda b, t: (0, 0, 0)
    return pl.pallas_call(
        functools.partial(_retention_kernel, tb=tb, chunk_decay=chunk_decay),
        grid=(bsz, nt),
        in_specs=[pl.BlockSpec((tb, MIX_WIDTH), row_map(cb(COL_CQ))),
                  pl.BlockSpec((tb, MIX_WIDTH), row_map(cb(COL_CK))),
                  pl.BlockSpec((tb, MIX_WIDTH), row_map(cb(COL_CV))),
                  pl.BlockSpec((tb, MIX_WIDTH), row_map(cb(COL_CG))),
                  pl.BlockSpec((tb, HEAD_DIM), lambda b, t: (t, 0)),
                  pl.BlockSpec((tb, HEAD_DIM), lambda b, t: (t, 0)),
                  pl.BlockSpec((N_HEADS, RET_CHUNK, RET_CHUNK), const3),
                  pl.BlockSpec((N_HEADS, RET_CHUNK, HEAD_DIM), const3),
                  pl.BlockSpec((N_HEADS, RET_CHUNK, HEAD_DIM), const3)],
        out_specs=pl.BlockSpec((tb, MIX_WIDTH), row_map(0)),
        out_shape=jax.ShapeDtypeStruct((n, MIX_WIDTH), BF16),
        scratch_shapes=[pltpu.VMEM((N_HEADS, HEAD_DIM, HEAD_DIM), F32)],
        compiler_params=_cparams(("parallel", "arbitrary")),
    )(pb, pb, pb, pb, cos_t, sin_t, dmask, qdec, kdec)


def _fox_cum_kernel(z_ref, bias_ref, o_ref):
    z = z_ref[...]
    hi = z.astype(BF16)
    rem = z - hi.astype(F32)
    mid = rem.astype(BF16)
    lo = (rem - mid.astype(F32)).astype(BF16)
    sel = jnp.where(lax.broadcasted_iota(jnp.int32, (SUBLANES, LANES), 0)
                    == lax.broadcasted_iota(jnp.int32, (SUBLANES, LANES), 1), 1.0, 0.0).astype(BF16)
    nt = (((1,), (1,)), ((), ()))
    zt = (lax.dot_general(sel, hi, nt, preferred_element_type=F32)
          + lax.dot_general(sel, mid, nt, preferred_element_type=F32)
          + lax.dot_general(sel, lo, nt, preferred_element_type=F32))
    x = _log_sigmoid(zt + bias_ref[...])
    lane = lax.broadcasted_iota(jnp.int32, x.shape, 1)
    s = 1
    while s < x.shape[1]:
        x = x + jnp.where(lane >= s, pltpu.roll(x, s, axis=1), 0.0)
        s *= 2
    o_ref[...] = x * (-LOG2E)


def _fox_neg_cum(pf, bias_col, bsz, seq):
    return pl.pallas_call(
        _fox_cum_kernel,
        grid=(bsz,),
        in_specs=[pl.BlockSpec((seq, LANES), lambda b: (b, MIX_WIDTH // LANES)),
                  pl.BlockSpec((SUBLANES, 1), lambda b: (0, 0))],
        out_specs=pl.BlockSpec((None, SUBLANES, seq), lambda b: (b, 0, 0)),
        out_shape=jax.ShapeDtypeStruct((bsz, SUBLANES, seq), F32),
        compiler_params=_cparams(("parallel",)),
    )(pf, bias_col)


def _fox_kernel(q_ref, k_ref, v_ref, nck_ref, o_ref, m_ref, acc_ref, *, nsub, scale):
    i = pl.program_id(2)
    tk = FOX_BLOCK
    m_ref[...] = jnp.full_like(m_ref, -jnp.inf)
    acc_ref[...] = jnp.zeros_like(acc_ref)
    qs = [(q_ref[a * tk:(a + 1) * tk, :].astype(F32) * (scale * LOG2E)).astype(BF16)
          for a in range(nsub)]
    ones = jnp.ones((tk, HEAD_DIM), BF16)
    causal = (lax.broadcasted_iota(jnp.int32, (tk, tk), 1)
              <= lax.broadcasted_iota(jnp.int32, (tk, tk), 0))

    def attend(a, kblk, masked):
        rows = pl.ds(pl.multiple_of(kblk * tk, tk), tk)
        s = lax.dot_general(qs[a], k_ref[rows, :], (((1,), (1,)), ((), ())),
                            preferred_element_type=F32) + nck_ref[kblk]
        if masked:
            s = jnp.where(causal, s, -jnp.inf)
        m_prev = m_ref[a]
        m_new = jnp.maximum(m_prev, jnp.max(s, axis=1, keepdims=True))
        p = jnp.exp2(s - jnp.concatenate([m_new] * (tk // LANES), axis=1)).astype(BF16)
        v_aug = jnp.concatenate([v_ref[rows, :], ones], axis=1)
        alpha = jnp.exp2(m_prev - m_new)
        acc_ref[a] = jnp.concatenate([alpha, alpha], axis=1) * acc_ref[a] + jnp.dot(
            p, v_aug, preferred_element_type=F32)
        m_ref[a] = m_new

    def below_diagonal(j, c):
        for a in range(nsub):
            attend(a, j, False)
        return c

    lax.fori_loop(0, i * nsub, below_diagonal, 0)
    for d in range(nsub):
        for a in range(d, nsub):
            attend(a, i * nsub + d, a == d)
    for a in range(nsub):
        acc = acc_ref[a]
        o_ref[a * tk:(a + 1) * tk, :] = (acc[:, :HEAD_DIM] / acc[:, HEAD_DIM:]).astype(o_ref.dtype)


def _fox(pb, neg_cum, bsz, seq, nsub):
    n = bsz * seq
    tq = nsub * FOX_BLOCK
    nq = seq // tq
    nkb = seq // FOX_BLOCK
    cq, ck, cv = COL_BQ // HEAD_DIM, COL_BK // HEAD_DIM, COL_BV // HEAD_DIM
    return pl.pallas_call(
        functools.partial(_fox_kernel, nsub=nsub, scale=HEAD_DIM ** -0.5),
        grid=(bsz, N_HEADS, nq),
        in_specs=[pl.BlockSpec((tq, HEAD_DIM), lambda b, h, i: (b * nq + i, cq + h)),
                  pl.BlockSpec((seq, HEAD_DIM), lambda b, h, i: (b, ck + h)),
                  pl.BlockSpec((seq, HEAD_DIM), lambda b, h, i: (b, cv + h)),
                  pl.BlockSpec((None, None, nkb, 1, FOX_BLOCK), lambda b, h, i: (b, h, 0, 0, 0))],
        out_specs=pl.BlockSpec((tq, HEAD_DIM), lambda b, h, i: (b * nq + i, h)),
        out_shape=jax.ShapeDtypeStruct((n, MIX_WIDTH), BF16),
        scratch_shapes=[pltpu.VMEM((nsub, FOX_BLOCK, LANES), F32),
                        pltpu.VMEM((nsub, FOX_BLOCK, 2 * HEAD_DIM), F32)],
        compiler_params=_cparams(("parallel", "parallel", "arbitrary")),
    )(pb, pb, pb, neg_cum.reshape(bsz, SUBLANES, nkb, 1, FOX_BLOCK))


def _layer_norm(x, g, b):
    mu = jnp.mean(x, axis=1, keepdims=True)
    c = x - mu
    var = jnp.mean(c * c, axis=1, keepdims=True)
    return c * lax.rsqrt(var + LN_EPS) * g + b


def _route(logits):
    lane = lax.broadcasted_iota(jnp.int32, logits.shape, 1)
    lane_f = lane.astype(F32)
    ninf = -jnp.inf
    big = float(LANES)
    gl = jnp.where(lane < N_GROUPS, logits, ninf)
    gmax = jnp.max(gl, axis=1, keepdims=True)
    gidx = jnp.min(jnp.where(gl == gmax, lane_f, big), axis=1, keepdims=True)
    gprob = 1.0 / jnp.sum(jnp.exp(gl - gmax), axis=1, keepdims=True)
    e_group = ((lane - N_GROUPS) // EXPERTS_PER_GROUP).astype(F32)
    in_grp = (lane >= N_GROUPS) & (lane < N_GROUPS + N_EXPERTS) & (e_group == gidx)
    el = jnp.where(in_grp, logits, ninf)
    t1 = jnp.max(el, axis=1, keepdims=True)
    i1 = jnp.min(jnp.where(el == t1, lane_f, big), axis=1, keepdims=True)
    el2 = jnp.where(lane_f == i1, ninf, el)
    t2 = jnp.max(el2, axis=1, keepdims=True)
    i2 = jnp.min(jnp.where(el2 == t2, lane_f, big), axis=1, keepdims=True)
    d = jnp.exp(t2 - t1)
    g1 = gprob / (1.0 + d)
    g2 = gprob * d / (1.0 + d)
    out = jnp.where(lane == 0, i1 - N_GROUPS, 0.0)
    out = jnp.where(lane == 1, i2 - N_GROUPS, out)
    out = jnp.where(lane == 2, g1, out)
    out = jnp.where(lane == 3, g2, out)
    return out


def _merge_kernel(x_ref, ya_ref, yb_ref, yc_ref, ga_ref, gb_ref, gc_ref, wb_ref, wo_ref,
                  lng_ref, lnb_ref, wrh_ref, wrl_ref, x1_ref, route_ref, *, alpha):
    def branch(y_ref, g_ref, idx):
        return jax.nn.sigmoid(g_ref[...].astype(F32)) * jnp.dot(
            y_ref[...], wb_ref[idx], preferred_element_type=F32)

    merged = branch(ya_ref, ga_ref, 0) + branch(yb_ref, gb_ref, 1) + branch(yc_ref, gc_ref, 2)
    mix = jnp.dot(merged.astype(BF16), wo_ref[...], preferred_element_type=F32)
    x1 = _layer_norm(alpha * x_ref[...] + mix, lng_ref[...], lnb_ref[...])
    x1_ref[...] = x1
    x_hi = x1.astype(BF16)
    x_lo = (x1 - x_hi.astype(F32)).astype(BF16)
    logits = (jnp.dot(x_hi, wrh_ref[...], preferred_element_type=F32)
              + jnp.dot(x_lo, wrh_ref[...], preferred_element_type=F32)
              + jnp.dot(x_hi, wrl_ref[...], preferred_element_type=F32))
    route_ref[...] = _route(logits)


def _merge(x2d, ya, yb, yc, pb, wb, wo, lng, lnb, wr_hi, wr_lo, alpha, tm):
    n = x2d.shape[0]
    row = lambda c: (lambda i: (i, c))
    const2 = lambda i: (0, 0)
    return pl.pallas_call(
        functools.partial(_merge_kernel, alpha=alpha),
        grid=(n // tm,),
        in_specs=[pl.BlockSpec((tm, D_MODEL), row(0)),
                  pl.BlockSpec((tm, MIX_WIDTH), row(0)),
                  pl.BlockSpec((tm, MIX_WIDTH), row(0)),
                  pl.BlockSpec((tm, MIX_WIDTH), row(0)),
                  pl.BlockSpec((tm, D_MODEL), row(COL_GA // D_MODEL)),
                  pl.BlockSpec((tm, D_MODEL), row(COL_GB // D_MODEL)),
                  pl.BlockSpec((tm, D_MODEL), row(COL_GC // D_MODEL)),
                  pl.BlockSpec((3, MIX_WIDTH, D_MODEL), lambda i: (0, 0, 0)),
                  pl.BlockSpec((D_MODEL, D_MODEL), const2),
                  pl.BlockSpec((1, D_MODEL), const2),
                  pl.BlockSpec((1, D_MODEL), const2),
                  pl.BlockSpec((D_MODEL, LANES), const2),
                  pl.BlockSpec((D_MODEL, LANES), const2)],
        out_specs=[pl.BlockSpec((tm, D_MODEL), row(0)),
                   pl.BlockSpec((tm, LANES), row(0))],
        out_shape=[jax.ShapeDtypeStruct((n, D_MODEL), F32),
                   jax.ShapeDtypeStruct((n, LANES), F32)],
        compiler_params=_cparams(("parallel",)),
    )(x2d, ya, yb, yc, pb, pb, pb, wb, wo, lng, lnb, wr_hi, wr_lo)


def _rank_kernel(route_ref, rank_ref, cnt_ref, carry_ref, *, tr):
    @pl.when(pl.program_id(0) == 0)
    def _():
        carry_ref[...] = jnp.zeros_like(carry_ref)

    r = route_ref[...]
    lane = lax.broadcasted_iota(jnp.int32, r.shape, 1)
    lane_f = lane.astype(F32)
    oh1 = lane_f == r[:, 0:1]
    oh2 = lane_f == r[:, 1:2]
    oh = jnp.where(oh1, 1.0, jnp.where(oh2, 1.0, 0.0))
    earlier = jnp.where(lax.broadcasted_iota(jnp.int32, (tr, tr), 1)
                        < lax.broadcasted_iota(jnp.int32, (tr, tr), 0), 1.0, 0.0).astype(BF16)
    before = jnp.dot(earlier, oh.astype(BF16), preferred_element_type=F32) + carry_ref[...]
    rank1 = jnp.sum(jnp.where(oh1, before, 0.0), axis=1, keepdims=True)
    rank2 = jnp.sum(jnp.where(oh2, before, 0.0), axis=1, keepdims=True)
    rank_ref[...] = jnp.where(lane == 0, rank1, jnp.where(lane == 1, rank2, 0.0))
    total = carry_ref[...] + jnp.sum(oh, axis=0, keepdims=True)
    carry_ref[...] = total
    cnt_ref[...] = total


def _rank(route, tr):
    n = route.shape[0]
    return pl.pallas_call(
        functools.partial(_rank_kernel, tr=tr),
        grid=(n // tr,),
        in_specs=[pl.BlockSpec((tr, LANES), lambda i: (i, 0))],
        out_specs=[pl.BlockSpec((tr, LANES), lambda i: (i, 0)),
                   pl.BlockSpec((1, LANES), lambda i: (0, 0))],
        out_shape=[jax.ShapeDtypeStruct((n, LANES), F32),
                   jax.ShapeDtypeStruct((1, LANES), F32)],
        scratch_shapes=[pltpu.VMEM((1, LANES), F32)],
        compiler_params=_cparams(("arbitrary",)),
    )(route)


def _dispatch_plan(route, rank, counts, n_tok):
    n_rows = n_tok * TOP_K + N_EXPERTS * MOE_ROWS
    n_blocks = n_rows // MOE_ROWS
    cnt = counts[0, :N_EXPERTS].astype(jnp.int32)
    padded = (cnt + MOE_ROWS - 1) // MOE_ROWS * MOE_ROWS
    padded_end = jnp.cumsum(padded)
    padded_start = padded_end - padded
    expert = route[:, :TOP_K].astype(jnp.int32)
    start = jnp.sum(jnp.where(expert[:, :, None] == jnp.arange(N_EXPERTS)[None, None, :],
                              padded_start[None, None, :], 0), axis=-1)
    dest = start + rank[:, :TOP_K].astype(jnp.int32)
    block_start = jnp.arange(n_blocks, dtype=jnp.int32) * MOE_ROWS
    block_expert = jnp.minimum(jnp.sum(block_start[:, None] >= padded_end[None, :], axis=1),
                               N_EXPERTS - 1).astype(jnp.int32)
    n_used = (padded_end[-1:] // MOE_ROWS).astype(jnp.int32)
    return dest, block_expert, n_used, padded_end.astype(jnp.int32), n_rows


def _dispatch_kernel(pend_ref, d0_ref, d1_ref, x_ref, xr_hbm, zbuf, sem_z, sem, *, tmb, n_blocks):
    i = pl.program_id(0)

    def zero_copy(e):
        end = pend_ref[e]
        return pltpu.make_async_copy(
            zbuf, xr_hbm.at[pl.ds(pl.multiple_of(end - MOE_ROWS, MOE_ROWS), MOE_ROWS), :], sem_z)

    def nonempty(e):
        return pend_ref[e] > (pend_ref[e - 1] if e > 0 else 0)

    @pl.when(i == 0)
    def _():
        zbuf[...] = jnp.zeros_like(zbuf)
        for e in range(N_EXPERTS):
            @pl.when(nonempty(e))
            def _():
                zero_copy(e).start()
        for e in range(N_EXPERTS):
            @pl.when(nonempty(e))
            def _():
                zero_copy(e).wait()

        def tail_copy(b):
            return pltpu.make_async_copy(
                zbuf, xr_hbm.at[pl.ds(pl.multiple_of(b * MOE_ROWS, MOE_ROWS), MOE_ROWS), :], sem_z)

        def tail_start(b, c):
            tail_copy(b).start()
            return c

        def tail_wait(b, c):
            tail_copy(b).wait()
            return c

        first_unused = pend_ref[N_EXPERTS - 1] // MOE_ROWS
        lax.fori_loop(first_unused, n_blocks, tail_start, 0)
        lax.fori_loop(first_unused, n_blocks, tail_wait, 0)

    def row_copy(r, dst):
        return pltpu.make_async_copy(x_ref.at[pl.ds(r, 1), :], xr_hbm.at[pl.ds(dst, 1), :], sem)

    def issue(g, c):
        for u in range(DMA_UNROLL):
            r = g * DMA_UNROLL + u
            row_copy(r, d0_ref[0, 0, r]).start()
            row_copy(r, d1_ref[0, 0, r]).start()
        return c

    lax.fori_loop(0, tmb // DMA_UNROLL, issue, 0)

    for _ in range(TOP_K):
        pltpu.make_async_copy(x_ref, xr_hbm.at[pl.ds(0, tmb), :], sem).wait()


def _dispatch(x1, dest, padded_end, n_rows, tmb):
    n = x1.shape[0]
    nb = n // tmb
    idx_spec = pl.BlockSpec((1, 1, tmb), lambda i, pe: (i, 0, 0), memory_space=pltpu.SMEM)
    grid_spec = pltpu.PrefetchScalarGridSpec(
        num_scalar_prefetch=1,
        grid=(nb,),
        in_specs=[idx_spec, idx_spec, pl.BlockSpec((tmb, D_MODEL), lambda i, pe: (i, 0))],
        out_specs=pl.BlockSpec(memory_space=pl.ANY),
        scratch_shapes=[pltpu.VMEM((MOE_ROWS, D_MODEL), F32),
                        pltpu.SemaphoreType.DMA, pltpu.SemaphoreType.DMA],
    )
    return pl.pallas_call(
        functools.partial(_dispatch_kernel, tmb=tmb, n_blocks=n_rows // MOE_ROWS),
        grid_spec=grid_spec,
        out_shape=jax.ShapeDtypeStruct((n_rows, D_MODEL), F32),
        compiler_params=_cparams(("arbitrary",)),
    )(padded_end, dest[:, 0].reshape(nb, 1, tmb), dest[:, 1].reshape(nb, 1, tmb), x1)


def _expert_kernel(bexp_ref, nused_ref, x_ref, wg_ref, wu_ref, wd_ref, y_ref):
    del bexp_ref
    i = pl.program_id(0)

    @pl.when(i < nused_ref[0])
    def _():
        xb = x_ref[...].astype(BF16)
        hg = jnp.dot(xb, wg_ref[0], preferred_element_type=F32)
        hu = jnp.dot(xb, wu_ref[0], preferred_element_type=F32)
        hid = (hg * jax.nn.sigmoid(hg)) * hu
        y_ref[...] = jnp.dot(hid.astype(BF16), wd_ref[0], preferred_element_type=F32)

    @pl.when(i >= nused_ref[0])
    def _():
        y_ref[...] = jnp.zeros_like(y_ref)


def _experts(x_rows, block_expert, n_used, wg, wu, wd):
    n_rows = x_rows.shape[0]
    n_blocks = n_rows // MOE_ROWS
    grid_spec = pltpu.PrefetchScalarGridSpec(
        num_scalar_prefetch=2,
        grid=(n_blocks,),
        in_specs=[pl.BlockSpec((MOE_ROWS, D_MODEL), lambda i, be, nu: (jnp.minimum(i, nu[0] - 1), 0)),
                  pl.BlockSpec((1, D_MODEL, D_FF_EXPERT), lambda i, be, nu: (be[i], 0, 0)),
                  pl.BlockSpec((1, D_MODEL, D_FF_EXPERT), lambda i, be, nu: (be[i], 0, 0)),
                  pl.BlockSpec((1, D_FF_EXPERT, D_MODEL), lambda i, be, nu: (be[i], 0, 0))],
        out_specs=pl.BlockSpec((MOE_ROWS, D_MODEL), lambda i, be, nu: (i, 0)),
    )
    return pl.pallas_call(
        _expert_kernel,
        grid_spec=grid_spec,
        out_shape=jax.ShapeDtypeStruct((n_rows, D_MODEL), F32),
        compiler_params=_cparams(("arbitrary",)),
    )(block_expert, n_used, x_rows, wg, wu, wd)


def _combine_kernel(d0_ref, d1_ref, d0n_ref, d1n_ref, x_ref, route_ref, lng_ref, lnb_ref, y_hbm,
                    o_ref, ybuf, sems, *, alpha, tc, nb):
    i = pl.program_id(0)

    def row_copy(src, slot, choice, r):
        return pltpu.make_async_copy(y_hbm.at[pl.ds(src, 1), :],
                                     ybuf.at[slot, choice, pl.ds(r, 1), :], sems.at[slot])

    def fetch(da_ref, db_ref, slot):
        def issue(g, c):
            for u in range(DMA_UNROLL):
                r = g * DMA_UNROLL + u
                row_copy(da_ref[0, 0, r], slot, 0, r).start()
                row_copy(db_ref[0, 0, r], slot, 1, r).start()
            return c

        lax.fori_loop(0, tc // DMA_UNROLL, issue, 0)

    @pl.when(i == 0)
    def _():
        fetch(d0_ref, d1_ref, 0)

    for slot in range(2):
        @pl.when((i + 1 < nb) & ((i + 1) % 2 == slot))
        def _():
            fetch(d0n_ref, d1n_ref, slot)

    slot = i % 2

    for choice in range(TOP_K):
        pltpu.make_async_copy(y_hbm.at[pl.ds(0, tc), :], ybuf.at[slot, choice], sems.at[slot]).wait()

    r = route_ref[...]
    ffn = r[:, 2:3] * ybuf[slot, 0] + r[:, 3:4] * ybuf[slot, 1]
    o_ref[...] = _layer_norm(alpha * x_ref[...] + ffn, lng_ref[...], lnb_ref[...])


def _combine(x1, y_rows, dest, route, lng, lnb, alpha, tc):
    n = x1.shape[0]
    nb = n // tc
    const2 = lambda i: (0, 0)
    cur = pl.BlockSpec((1, 1, tc), lambda i: (i, 0, 0), memory_space=pltpu.SMEM)
    nxt = pl.BlockSpec((1, 1, tc), lambda i: (jnp.minimum(i + 1, nb - 1), 0, 0), memory_space=pltpu.SMEM)
    d0 = dest[:, 0].reshape(nb, 1, tc)
    d1 = dest[:, 1].reshape(nb, 1, tc)
    return pl.pallas_call(
        functools.partial(_combine_kernel, alpha=alpha, tc=tc, nb=nb),
        grid=(nb,),
        in_specs=[cur, cur, nxt, nxt,
                  pl.BlockSpec((tc, D_MODEL), lambda i: (i, 0)),
                  pl.BlockSpec((tc, LANES), lambda i: (i, 0)),
                  pl.BlockSpec((1, D_MODEL), const2),
                  pl.BlockSpec((1, D_MODEL), const2),
                  pl.BlockSpec(memory_space=pl.ANY)],
        out_specs=pl.BlockSpec((tc, D_MODEL), lambda i: (i, 0)),
        out_shape=jax.ShapeDtypeStruct((n, D_MODEL), F32),
        scratch_shapes=[pltpu.VMEM((2, TOP_K, tc, D_MODEL), F32),
                        pltpu.SemaphoreType.DMA((2,))],
        compiler_params=_cparams(("arbitrary",)),
    )(d0, d1, d0, d1, x1, route, lng, lnb, y_rows)


def _take_cols_kernel(src_ref, a_ref, b_ref, o_ref, *, shift):
    del src_ref
    if shift == 0:
        o_ref[...] = a_ref[...].astype(o_ref.dtype)
    else:
        lane = lax.broadcasted_iota(jnp.int32, a_ref.shape, 1)
        o_ref[...] = jnp.where(lane < LANES - shift,
                               pltpu.roll(a_ref[...], LANES - shift, axis=1),
                               pltpu.roll(b_ref[...], LANES - shift, axis=1)).astype(o_ref.dtype)


def _take_cols(w_all, layer, src_tiles, shift):
    d = w_all.shape[1]
    last = (w_all.shape[2] - 1) // LANES
    src = jnp.asarray(np.asarray(src_tiles, np.int32))
    grid_spec = pltpu.PrefetchScalarGridSpec(
        num_scalar_prefetch=1,
        grid=(len(src_tiles),),
        in_specs=[pl.BlockSpec((None, d, LANES), lambda j, s: (layer, 0, s[j])),
                  pl.BlockSpec((None, d, LANES), lambda j, s: (layer, 0, jnp.minimum(s[j] + 1, last)))],
        out_specs=pl.BlockSpec((d, LANES), lambda j, s: (0, j)),
    )
    return pl.pallas_call(
        functools.partial(_take_cols_kernel, shift=shift),
        grid_spec=grid_spec,
        out_shape=jax.ShapeDtypeStruct((d, len(src_tiles) * LANES), BF16),
        compiler_params=_cparams(("arbitrary",)),
    )(src, w_all, w_all)


def _permute_w_in(w_all, layer):
    per = MIX_WIDTH // LANES
    tiles = lambda t0, nt: list(range(t0, t0 + nt))
    pre = 7 * per
    aligned = tiles(0, per) + tiles(2 * per, 2 * per) + tiles(4 * per, 3 * per)
    gates = tiles(pre + 4 * per, 3 * D_MODEL // LANES)
    c_part = tiles(pre, 4 * per)
    main = jnp.concatenate([_take_cols(w_all, layer, gates, N_HEADS),
                            _take_cols(w_all, layer, aligned, 0),
                            _take_cols(w_all, layer, c_part, N_HEADS)], axis=1)
    fpart = _take_cols(w_all, layer, tiles(per, per) + [pre], 0)
    return main, fpart


def kernel(x, w_in, w_branch, w_out, fox_fgate_bias, hgrn_lb_logits, ln1_g, ln1_b,
           w_router_group, w_router_expert, w_up, w_gate, w_down, ln2_g, ln2_b):
    bsz, seq, d = x.shape
    depth = w_in.shape[0]
    n = bsz * seq
    alpha = float((2 * depth) ** 0.25)
    tm = min(1024, n)
    tb = min(512, seq)
    fox_sub = max(s for s in (1, 2, 4) if seq % (s * FOX_BLOCK) == 0)

    lb_cum = jnp.cumsum(jax.nn.softmax(hgrn_lb_logits.astype(F32), axis=0), axis=0)
    lower_bounds = lb_cum - lb_cum[0]
    tables = _retention_tables(seq)

    h = x.reshape(n, d)
    for layer in range(depth):
        w_main, w_f = _permute_w_in(w_in, layer)
        pb, pf = _project(h, w_main, w_f, tm, 1024)

        lb = lower_bounds[layer][None, :]
        ya = _hgrn2(pb, pf, jnp.log(lb), jnp.log1p(-lb), bsz, seq, tb)

        bias_col = jnp.concatenate([fox_fgate_bias[layer].astype(F32),
                                    jnp.zeros((SUBLANES - N_HEADS,), F32)])[:, None]
        neg_cum = _fox_neg_cum(pf, bias_col, bsz, seq)
        yb = _fox(pb, neg_cum, bsz, seq, fox_sub)

        yc = _retention(pb, tables, bsz, seq, tb)

        w_route = jnp.concatenate(
            [w_router_group[layer], w_router_expert[layer],
             jnp.zeros((d, LANES - N_GROUPS - N_EXPERTS), F32)], axis=1).astype(F32)
        wr_hi = w_route.astype(BF16)
        wr_lo = (w_route - wr_hi.astype(F32)).astype(BF16)
        x1, route = _merge(h, ya, yb, yc, pb, w_branch[layer].astype(BF16), w_out[layer].astype(BF16),
                           ln1_g[layer][None, :], ln1_b[layer][None, :], wr_hi, wr_lo, alpha, min(512, n))

        rank, counts = _rank(route, min(1024, n))
        dest, block_expert, n_used, padded_end, n_rows = _dispatch_plan(route, rank, counts, n)
        x_rows = _dispatch(x1, dest, padded_end, n_rows, min(2048, n))
        y_rows = _experts(x_rows, block_expert, n_used, w_gate[layer].astype(BF16),
                          w_up[layer].astype(BF16), w_down[layer].astype(BF16))
        h = _combine(x1, y_rows, dest, route, ln2_g[layer][None, :], ln2_b[layer][None, :], alpha,
                     min(512, n))
    return h.reshape(bsz, seq, d)
```

```python
import functools
import math

import numpy as np
import jax
import jax.numpy as jnp
from jax import lax
from jax.experimental import pallas as pl
from jax.experimental.pallas import tpu as pltpu

D_MODEL = 1024
HEAD_DIM = 128
MIX_WIDTH = D_MODEL // 2
N_HEADS = MIX_WIDTH // HEAD_DIM
N_GROUPS = 4
EXPERTS_PER_GROUP = 8
N_EXPERTS = N_GROUPS * EXPERTS_PER_GROUP
TOP_K = 2
D_FF_EXPERT = D_MODEL // 2
LN_EPS = 1e-5
HEAD_NORM_EPS = 1e-6
RET_ROPE_BASE = 10000.0

LANES = 128
SUBLANES = 8
HGRN_SUB = 2 * SUBLANES
HGRN_UNROLL = 8
RET_CHUNK = 256
FOX_BLOCK = 512
MOE_ROWS = 512
DMA_UNROLL = 8
VMEM_LIMIT = 48 * 1024 * 1024
LOG2E = 1.4426950408889634

COL_GA, COL_GB, COL_GC = 0, 1024, 2048
COL_AQ, COL_AI, COL_AG = 3072, 3584, 4096
COL_BQ, COL_BK, COL_BV = 4608, 5120, 5632
COL_CQ, COL_CK, COL_CV, COL_CG = 6144, 6656, 7168, 7680
N_MAIN = 8192
N_FGATE = MIX_WIDTH + LANES

F32 = jnp.float32
BF16 = jnp.bfloat16


def _cparams(sem):
    return pltpu.CompilerParams(dimension_semantics=sem, vmem_limit_bytes=VMEM_LIMIT)


def _proj_kernel(x_ref, w_ref, wf_ref, o_ref, of_ref, *, n_main):
    j = pl.program_id(1)
    xb = x_ref[...].astype(BF16)

    @pl.when(j < n_main)
    def _():
        o_ref[...] = jnp.dot(xb, w_ref[...], preferred_element_type=F32).astype(o_ref.dtype)

    @pl.when(j == n_main)
    def _():
        of_ref[...] = jnp.dot(xb, wf_ref[...], preferred_element_type=F32)


def _project(x2d, w, wf, tm, tn):
    n, d = x2d.shape
    c = w.shape[1]
    cf = wf.shape[1]
    n_main = c // tn
    last = n_main - 1
    return pl.pallas_call(
        functools.partial(_proj_kernel, n_main=n_main),
        grid=(n // tm, n_main + 1),
        in_specs=[pl.BlockSpec((tm, d), lambda i, j: (i, 0)),
                  pl.BlockSpec((d, tn), lambda i, j: (0, jnp.minimum(j, last))),
                  pl.BlockSpec((d, cf), lambda i, j: (0, 0))],
        out_specs=[pl.BlockSpec((tm, tn), lambda i, j: (i, jnp.minimum(j, last))),
                   pl.BlockSpec((tm, cf), lambda i, j: (i, 0))],
        out_shape=[jax.ShapeDtypeStruct((n, c), BF16),
                   jax.ShapeDtypeStruct((n, cf), F32)],
        compiler_params=_cparams(("parallel", "arbitrary")),
    )(x2d, w, wf)


def _log_sigmoid(z):
    return jnp.minimum(z, 0.0) - jnp.log1p(jnp.exp(-jnp.abs(z)))


def _hgrn2_kernel(q_ref, i_ref, g_ref, f_ref, loglb_ref, log1m_ref, o_ref, st_ref, rows_ref, *, tb):
    @pl.when(pl.program_id(1) == 0)
    def _():
        st_ref[...] = jnp.zeros_like(st_ref)

    row = lax.broadcasted_iota(jnp.int32, (HGRN_SUB, HEAD_DIM), 0)
    row8 = lax.broadcasted_iota(jnp.int32, (SUBLANES, HEAD_DIM), 0)
    ninf = -jnp.inf
    ones_sq = jnp.ones((HEAD_DIM, HEAD_DIM), BF16)

    def head_step(rows, h, u):
        cols = slice(h * HEAD_DIM, (h + 1) * HEAD_DIM)
        z = f_ref[rows, cols]
        a = loglb_ref[:, cols]
        b = log1m_ref[:, cols] + _log_sigmoid(z)
        log_f = jnp.maximum(a, b) + jnp.log1p(jnp.exp(-jnp.abs(a - b)))
        k = 1.0 - jnp.exp(log_f)
        q = q_ref[rows, cols].astype(F32)
        v = i_ref[rows, cols].astype(F32)
        cum = log_f * LOG2E
        for s in (1, 2, 4, 8):
            cum = cum + jnp.where(row >= s, pltpu.roll(cum, s, axis=0), 0.0)
        q_t, q_b = q[:SUBLANES], q[SUBLANES:]
        c_t, c_b = cum[:SUBLANES], cum[SUBLANES:]
        rows_ref[u, h, 0] = cum
        rows_ref[u, h, 1] = k
        rows_ref[u, h, 2] = v
        w_top, w_bot = [], []
        for s in range(HGRN_SUB):
            cs, ks = rows_ref[u, h, 0, s:s + 1, :], rows_ref[u, h, 1, s:s + 1, :]
            if s < SUBLANES:
                w_bot.append(q_b * ks * jnp.exp2(c_b - cs))
                e_t = c_t - cs
                if s > 0:
                    e_t = jnp.where(row8 >= s, e_t, ninf)
                w_top.append(q_t * ks * jnp.exp2(e_t))
            else:
                e_b = c_b - cs
                if s > SUBLANES:
                    e_b = jnp.where(row8 >= s - SUBLANES, e_b, ninf)
                w_bot.append(q_b * ks * jnp.exp2(e_b))
        scores = jnp.dot(jnp.concatenate(w_top + w_bot, axis=0).astype(BF16), ones_sq,
                         preferred_element_type=F32)
        o_t = jnp.zeros((SUBLANES, HEAD_DIM), F32)
        o_b = jnp.zeros((SUBLANES, HEAD_DIM), F32)
        for s in range(HGRN_SUB):
            vs = rows_ref[u, h, 2, s:s + 1, :]
            if s < SUBLANES:
                o_t = o_t + scores[s * SUBLANES:(s + 1) * SUBLANES] * vs
            b0 = (SUBLANES + s) * SUBLANES
            o_b = o_b + scores[b0:b0 + SUBLANES] * vs
        o = jnp.concatenate([o_t, o_b], axis=0)
        st = st_ref[h]
        qd = (q * jnp.exp2(cum)).astype(BF16)
        o = o + lax.dot_general(qd, st.astype(BF16), (((1,), (1,)), ((), ())),
                                preferred_element_type=F32)
        last = cum[HGRN_SUB - 1:HGRN_SUB, :]
        kd = (k * jnp.exp2(last - cum)).astype(BF16)
        upd = lax.dot_general(v.astype(BF16), kd, (((0,), (0,)), ((), ())),
                              preferred_element_type=F32)
        st_ref[h] = st * jnp.exp2(last) + upd
        y = o * lax.rsqrt(jnp.mean(o * o, axis=1, keepdims=True) + HEAD_NORM_EPS)
        y = y * jax.nn.sigmoid(g_ref[rows, cols].astype(F32))
        o_ref[rows, cols] = y.astype(o_ref.dtype)

    def step(j, carry):
        for u in range(HGRN_UNROLL):
            r0 = pl.multiple_of((j * HGRN_UNROLL + u) * HGRN_SUB, HGRN_SUB)
            for h in range(N_HEADS):
                head_step(pl.ds(r0, HGRN_SUB), h, u)
        return carry

    lax.fori_loop(0, tb // (HGRN_SUB * HGRN_UNROLL), step, 0)


def _hgrn2(pb, pf, log_lb, log1m_lb, bsz, seq, tb):
    n = bsz * seq
    nt = seq // tb
    cb = lambda c: c // MIX_WIDTH
    row_map = lambda c: (lambda b, t: (b * nt + t, c))
    return pl.pallas_call(
        functools.partial(_hgrn2_kernel, tb=tb),
        grid=(bsz, nt),
        in_specs=[pl.BlockSpec((tb, MIX_WIDTH), row_map(cb(COL_AQ))),
                  pl.BlockSpec((tb, MIX_WIDTH), row_map(cb(COL_AI))),
                  pl.BlockSpec((tb, MIX_WIDTH), row_map(cb(COL_AG))),
                  pl.BlockSpec((tb, MIX_WIDTH), row_map(0)),
                  pl.BlockSpec((1, MIX_WIDTH), lambda b, t: (0, 0)),
                  pl.BlockSpec((1, MIX_WIDTH), lambda b, t: (0, 0))],
        out_specs=pl.BlockSpec((tb, MIX_WIDTH), row_map(0)),
        out_shape=jax.ShapeDtypeStruct((n, MIX_WIDTH), BF16),
        scratch_shapes=[pltpu.VMEM((N_HEADS, HEAD_DIM, HEAD_DIM), F32),
                        pltpu.VMEM((HGRN_UNROLL, N_HEADS, 3, HGRN_SUB, HEAD_DIM), F32)],
        compiler_params=_cparams(("parallel", "arbitrary")),
    )(pb, pb, pb, pf, log_lb, log1m_lb)


def _retention_kernel(q_ref, k_ref, v_ref, g_ref, cos_ref, sin_ref, dmask_ref, qdec_ref, kdec_ref,
                      o_ref, st_ref, *, tb, chunk_decay):
    @pl.when(pl.program_id(1) == 0)
    def _():
        st_ref[...] = jnp.zeros_like(st_ref)

    def step(j, carry):
        r0 = pl.multiple_of(j * RET_CHUNK, RET_CHUNK)
        rows = pl.ds(r0, RET_CHUNK)
        cos = cos_ref[rows, :]
        sin = sin_ref[rows, :]
        for h in range(N_HEADS):
            cols = slice(h * HEAD_DIM, (h + 1) * HEAD_DIM)
            q = q_ref[rows, cols].astype(F32)
            k = k_ref[rows, cols].astype(F32)
            q = q * cos + pltpu.roll(q, HEAD_DIM // 2, axis=1) * sin
            k = (k * cos + pltpu.roll(k, HEAD_DIM // 2, axis=1) * sin) * (HEAD_DIM ** -0.5)
            v = v_ref[rows, cols]
            inner = lax.dot_general(q.astype(BF16), k.astype(BF16), (((1,), (1,)), ((), ())),
                                    preferred_element_type=F32) * dmask_ref[h]
            st = st_ref[h]
            o = (jnp.dot(inner.astype(BF16), v, preferred_element_type=F32)
                 + jnp.dot((q * qdec_ref[h]).astype(BF16), st.astype(BF16),
                           preferred_element_type=F32))
            upd = lax.dot_general((k * kdec_ref[h]).astype(BF16), v, (((0,), (0,)), ((), ())),
                                  preferred_element_type=F32)
            st_ref[h] = st * chunk_decay[h] + upd
            c = o - jnp.mean(o, axis=1, keepdims=True)
            y = c * lax.rsqrt(jnp.mean(c * c, axis=1, keepdims=True) + HEAD_NORM_EPS)
            g = g_ref[rows, cols].astype(F32)
            o_ref[rows, cols] = (y * (g * jax.nn.sigmoid(g))).astype(o_ref.dtype)
        return carry

    lax.fori_loop(0, tb // RET_CHUNK, step, 0)


def _retention_tables(seq):
    half = HEAD_DIM // 2
    inv = 1.0 / (RET_ROPE_BASE ** jnp.linspace(0.0, 1.0, half, dtype=F32))
    ang = jnp.arange(seq, dtype=F32)[:, None] * inv[None, :]
    cos = jnp.cos(ang)
    sin = jnp.sin(ang)
    cos_t = jnp.concatenate([cos, cos], axis=-1)
    sin_t = jnp.concatenate([-sin, sin], axis=-1)
    log_gamma = jnp.log(1.0 - jnp.power(2.0, -5.0 - jnp.arange(N_HEADS, dtype=F32)))
    idx = jnp.arange(RET_CHUNK, dtype=F32)
    rel = idx[:, None] - idx[None, :]
    dmask = jnp.where(rel >= 0, jnp.exp(log_gamma[:, None, None] * jnp.maximum(rel, 0.0)), 0.0)
    ones = jnp.ones((1, 1, HEAD_DIM), F32)
    qdec = jnp.exp(log_gamma[:, None] * (idx + 1.0))[..., None] * ones
    kdec = jnp.exp(log_gamma[:, None] * (RET_CHUNK - 1.0 - idx))[..., None] * ones
    return cos_t, sin_t, dmask, qdec, kdec


def _retention(pb, tables, bsz, seq, tb):
    n = bsz * seq
    nt = seq // tb
    cos_t, sin_t, dmask, qdec, kdec = tables
    chunk_decay = tuple(float((1.0 - 2.0 ** (-5.0 - h)) ** RET_CHUNK) for h in range(N_HEADS))
    cb = lambda c: c // MIX_WIDTH
    row_map = lambda c: (lambda b, t: (b * nt + t, c))
    const3 = lambda b, t: (0, 0, 0)
    return pl.pallas_call(
        functools.partial(_retention_kernel, tb=tb, chunk_decay=chunk_decay),
        grid=(bsz, nt),
        in_specs=[pl.BlockSpec((tb, MIX_WIDTH), row_map(cb(COL_CQ))),
                  pl.BlockSpec((tb, MIX_WIDTH), row_map(cb(COL_CK))),
                  pl.BlockSpec((tb, MIX_WIDTH), row_map(cb(COL_CV))),
                  pl.BlockSpec((tb, MIX_WIDTH), row_map(cb(COL_CG))),
                  pl.BlockSpec((tb, HEAD_DIM), lambda b, t: (t, 0)),
                  pl.BlockSpec((tb, HEAD_DIM), lambda b, t: (t, 0)),
                  pl.BlockSpec((N_HEADS, RET_CHUNK, RET_CHUNK), const3),
                  pl.BlockSpec((N_HEADS, RET_CHUNK, HEAD_DIM), const3),
                  pl.BlockSpec((N_HEADS, RET_CHUNK, HEAD_DIM), const3)],
        out_specs=pl.BlockSpec((tb, MIX_WIDTH), row_map(0)),
        out_shape=jax.ShapeDtypeStruct((n, MIX_WIDTH), BF16),
        scratch_shapes=[pltpu.VMEM((N_HEADS, HEAD_DIM, HEAD_DIM), F32)],
        compiler_params=_cparams(("parallel", "arbitrary")),
    )(pb, pb, pb, pb, cos_t, sin_t, dmask, qdec, kdec)


def _fox_cum_kernel(z_ref, bias_ref, o_ref):
    z = z_ref[...]
    hi = z.astype(BF16)
    rem = z - hi.astype(F32)
    mid = rem.astype(BF16)
    lo = (rem - mid.astype(F32)).astype(BF16)
    sel = jnp.where(lax.broadcasted_iota(jnp.int32, (SUBLANES, LANES), 0)
                    == lax.broadcasted_iota(jnp.int32, (SUBLANES, LANES), 1), 1.0, 0.0).astype(BF16)
    nt = (((1,), (1,)), ((), ()))
    zt = (lax.dot_general(sel, hi, nt, preferred_element_type=F32)
          + lax.dot_general(sel, mid, nt, preferred_element_type=F32)
          + lax.dot_general(sel, lo, nt, preferred_element_type=F32))
    x = _log_sigmoid(zt + bias_ref[...])
    lane = lax.broadcasted_iota(jnp.int32, x.shape, 1)
    s = 1
    while s < x.shape[1]:
        x = x + jnp.where(lane >= s, pltpu.roll(x, s, axis=1), 0.0)
        s *= 2
    o_ref[...] = x * (-LOG2E)


def _fox_neg_cum(pf, bias_col, bsz, seq):
    return pl.pallas_call(
        _fox_cum_kernel,
        grid=(bsz,),
        in_specs=[pl.BlockSpec((seq, LANES), lambda b: (b, MIX_WIDTH // LANES)),
                  pl.BlockSpec((SUBLANES, 1), lambda b: (0, 0))],
        out_specs=pl.BlockSpec((None, SUBLANES, seq), lambda b: (b, 0, 0)),
        out_shape=jax.ShapeDtypeStruct((bsz, SUBLANES, seq), F32),
        compiler_params=_cparams(("parallel",)),
    )(pf, bias_col)


def _fox_kernel(q_ref, k_ref, v_ref, nck_ref, o_ref, m_ref, acc_ref, *, nsub, scale):
    i = pl.program_id(2)
    tk = FOX_BLOCK
    m_ref[...] = jnp.full_like(m_ref, -jnp.inf)
    acc_ref[...] = jnp.zeros_like(acc_ref)
    qs = [(q_ref[a * tk:(a + 1) * tk, :].astype(F32) * (scale * LOG2E)).astype(BF16)
          for a in range(nsub)]
    ones = jnp.ones((tk, HEAD_DIM), BF16)
    causal = (lax.broadcasted_iota(jnp.int32, (tk, tk), 1)
              <= lax.broadcasted_iota(jnp.int32, (tk, tk), 0))

    def attend(a, kblk, masked):
        rows = pl.ds(pl.multiple_of(kblk * tk, tk), tk)
        s = lax.dot_general(qs[a], k_ref[rows, :], (((1,), (1,)), ((), ())),
                            preferred_element_type=F32) + nck_ref[kblk]
        if masked:
            s = jnp.where(causal, s, -jnp.inf)
        m_prev = m_ref[a]
        m_new = jnp.maximum(m_prev, jnp.max(s, axis=1, keepdims=True))
        p = jnp.exp2(s - jnp.concatenate([m_new] * (tk // LANES), axis=1)).astype(BF16)
        v_aug = jnp.concatenate([v_ref[rows, :], ones], axis=1)
        alpha = jnp.exp2(m_prev - m_new)
        acc_ref[a] = jnp.concatenate([alpha, alpha], axis=1) * acc_ref[a] + jnp.dot(
            p, v_aug, preferred_element_type=F32)
        m_ref[a] = m_new

    def below_diagonal(j, c):
        for a in range(nsub):
            attend(a, j, False)
        return c

    lax.fori_loop(0, i * nsub, below_diagonal, 0)
    for d in range(nsub):
        for a in range(d, nsub):
            attend(a, i * nsub + d, a == d)
    for a in range(nsub):
        acc = acc_ref[a]
        o_ref[a * tk:(a + 1) * tk, :] = (acc[:, :HEAD_DIM] / acc[:, HEAD_DIM:]).astype(o_ref.dtype)


def _fox(pb, neg_cum, bsz, seq, nsub):
    n = bsz * seq
    tq = nsub * FOX_BLOCK
    nq = seq // tq
    nkb = seq // FOX_BLOCK
    cq, ck, cv = COL_BQ // HEAD_DIM, COL_BK // HEAD_DIM, COL_BV // HEAD_DIM
    return pl.pallas_call(
        functools.partial(_fox_kernel, nsub=nsub, scale=HEAD_DIM ** -0.5),
        grid=(bsz, N_HEADS, nq),
        in_specs=[pl.BlockSpec((tq, HEAD_DIM), lambda b, h, i: (b * nq + i, cq + h)),
                  pl.BlockSpec((seq, HEAD_DIM), lambda b, h, i: (b, ck + h)),
                  pl.BlockSpec((seq, HEAD_DIM), lambda b, h, i: (b, cv + h)),
                  pl.BlockSpec((None, None, nkb, 1, FOX_BLOCK), lambda b, h, i: (b, h, 0, 0, 0))],
        out_specs=pl.BlockSpec((tq, HEAD_DIM), lambda b, h, i: (b * nq + i, h)),
        out_shape=jax.ShapeDtypeStruct((n, MIX_WIDTH), BF16),
        scratch_shapes=[pltpu.VMEM((nsub, FOX_BLOCK, LANES), F32),
                        pltpu.VMEM((nsub, FOX_BLOCK, 2 * HEAD_DIM), F32)],
        compiler_params=_cparams(("parallel", "parallel", "arbitrary")),
    )(pb, pb, pb, neg_cum.reshape(bsz, SUBLANES, nkb, 1, FOX_BLOCK))


def _layer_norm(x, g, b):
    mu = jnp.mean(x, axis=1, keepdims=True)
    c = x - mu
    var = jnp.mean(c * c, axis=1, keepdims=True)
    return c * lax.rsqrt(var + LN_EPS) * g + b


def _route(logits):
    lane = lax.broadcasted_iota(jnp.int32, logits.shape, 1)
    lane_f = lane.astype(F32)
    ninf = -jnp.inf
    big = float(LANES)
    gl = jnp.where(lane < N_GROUPS, logits, ninf)
    gmax = jnp.max(gl, axis=1, keepdims=True)
    gidx = jnp.min(jnp.where(gl == gmax, lane_f, big), axis=1, keepdims=True)
    gprob = 1.0 / jnp.sum(jnp.exp(gl - gmax), axis=1, keepdims=True)
    e_group = ((lane - N_GROUPS) // EXPERTS_PER_GROUP).astype(F32)
    in_grp = (lane >= N_GROUPS) & (lane < N_GROUPS + N_EXPERTS) & (e_group == gidx)
    el = jnp.where(in_grp, logits, ninf)
    t1 = jnp.max(el, axis=1, keepdims=True)
    i1 = jnp.min(jnp.where(el == t1, lane_f, big), axis=1, keepdims=True)
    el2 = jnp.where(lane_f == i1, ninf, el)
    t2 = jnp.max(el2, axis=1, keepdims=True)
    i2 = jnp.min(jnp.where(el2 == t2, lane_f, big), axis=1, keepdims=True)
    d = jnp.exp(t2 - t1)
    g1 = gprob / (1.0 + d)
    g2 = gprob * d / (1.0 + d)
    out = jnp.where(lane == 0, i1 - N_GROUPS, 0.0)
    out = jnp.where(lane == 1, i2 - N_GROUPS, out)
    out = jnp.where(lane == 2, g1, out)
    out = jnp.where(lane == 3, g2, out)
    return out


def _merge_kernel(x_ref, ya_ref, yb_ref, yc_ref, ga_ref, gb_ref, gc_ref, wb_ref, wo_ref,
                  lng_ref, lnb_ref, wrh_ref, wrl_ref, x1_ref, route_ref, *, alpha):
    def branch(y_ref, g_ref, idx):
        return jax.nn.sigmoid(g_ref[...].astype(F32)) * jnp.dot(
            y_ref[...], wb_ref[idx], preferred_element_type=F32)

    merged = branch(ya_ref, ga_ref, 0) + branch(yb_ref, gb_ref, 1) + branch(yc_ref, gc_ref, 2)
    mix = jnp.dot(merged.astype(BF16), wo_ref[...], preferred_element_type=F32)
    x1 = _layer_norm(alpha * x_ref[...] + mix, lng_ref[...], lnb_ref[...])
    x1_ref[...] = x1
    x_hi = x1.astype(BF16)
    x_lo = (x1 - x_hi.astype(F32)).astype(BF16)
    logits = (jnp.dot(x_hi, wrh_ref[...], preferred_element_type=F32)
              + jnp.dot(x_lo, wrh_ref[...], preferred_element_type=F32)
              + jnp.dot(x_hi, wrl_ref[...], preferred_element_type=F32))
    route_ref[...] = _route(logits)


def _merge(x2d, ya, yb, yc, pb, wb, wo, lng, lnb, wr_hi, wr_lo, alpha, tm):
    n = x2d.shape[0]
    row = lambda c: (lambda i: (i, c))
    const2 = lambda i: (0, 0)
    return pl.pallas_call(
        functools.partial(_merge_kernel, alpha=alpha),
        grid=(n // tm,),
        in_specs=[pl.BlockSpec((tm, D_MODEL), row(0)),
                  pl.BlockSpec((tm, MIX_WIDTH), row(0)),
                  pl.BlockSpec((tm, MIX_WIDTH), row(0)),
                  pl.BlockSpec((tm, MIX_WIDTH), row(0)),
                  pl.BlockSpec((tm, D_MODEL), row(COL_GA // D_MODEL)),
                  pl.BlockSpec((tm, D_MODEL), row(COL_GB // D_MODEL)),
                  pl.BlockSpec((tm, D_MODEL), row(COL_GC // D_MODEL)),
                  pl.BlockSpec((3, MIX_WIDTH, D_MODEL), lambda i: (0, 0, 0)),
                  pl.BlockSpec((D_MODEL, D_MODEL), const2),
                  pl.BlockSpec((1, D_MODEL), const2),
                  pl.BlockSpec((1, D_MODEL), const2),
                  pl.BlockSpec((D_MODEL, LANES), const2),
                  pl.BlockSpec((D_MODEL, LANES), const2)],
        out_specs=[pl.BlockSpec((tm, D_MODEL), row(0)),
                   pl.BlockSpec((tm, LANES), row(0))],
        out_shape=[jax.ShapeDtypeStruct((n, D_MODEL), F32),
                   jax.ShapeDtypeStruct((n, LANES), F32)],
        compiler_params=_cparams(("parallel",)),
    )(x2d, ya, yb, yc, pb, pb, pb, wb, wo, lng, lnb, wr_hi, wr_lo)


def _rank_kernel(route_ref, rank_ref, cnt_ref, carry_ref, *, tr):
    @pl.when(pl.program_id(0) == 0)
    def _():
        carry_ref[...] = jnp.zeros_like(carry_ref)

    r = route_ref[...]
    lane = lax.broadcasted_iota(jnp.int32, r.shape, 1)
    lane_f = lane.astype(F32)
    oh1 = lane_f == r[:, 0:1]
    oh2 = lane_f == r[:, 1:2]
    oh = jnp.where(oh1, 1.0, jnp.where(oh2, 1.0, 0.0))
    earlier = jnp.where(lax.broadcasted_iota(jnp.int32, (tr, tr), 1)
                        < lax.broadcasted_iota(jnp.int32, (tr, tr), 0), 1.0, 0.0).astype(BF16)
    before = jnp.dot(earlier, oh.astype(BF16), preferred_element_type=F32) + carry_ref[...]
    rank1 = jnp.sum(jnp.where(oh1, before, 0.0), axis=1, keepdims=True)
    rank2 = jnp.sum(jnp.where(oh2, before, 0.0), axis=1, keepdims=True)
    rank_ref[...] = jnp.where(lane == 0, rank1, jnp.where(lane == 1, rank2, 0.0))
    total = carry_ref[...] + jnp.sum(oh, axis=0, keepdims=True)
    carry_ref[...] = total
    cnt_ref[...] = total


def _rank(route, tr):
    n = route.shape[0]
    return pl.pallas_call(
        functools.partial(_rank_kernel, tr=tr),
        grid=(n // tr,),
        in_specs=[pl.BlockSpec((tr, LANES), lambda i: (i, 0))],
        out_specs=[pl.BlockSpec((tr, LANES), lambda i: (i, 0)),
                   pl.BlockSpec((1, LANES), lambda i: (0, 0))],
        out_shape=[jax.ShapeDtypeStruct((n, LANES), F32),
                   jax.ShapeDtypeStruct((1, LANES), F32)],
        scratch_shapes=[pltpu.VMEM((1, LANES), F32)],
        compiler_params=_cparams(("arbitrary",)),
    )(route)


def _dispatch_plan(route, rank, counts, n_tok):
    n_rows = n_tok * TOP_K + N_EXPERTS * MOE_ROWS
    n_blocks = n_rows // MOE_ROWS
    cnt = counts[0, :N_EXPERTS].astype(jnp.int32)
    padded = (cnt + MOE_ROWS - 1) // MOE_ROWS * MOE_ROWS
    padded_end = jnp.cumsum(padded)
    padded_start = padded_end - padded
    expert = route[:, :TOP_K].astype(jnp.int32)
    start = jnp.sum(jnp.where(expert[:, :, None] == jnp.arange(N_EXPERTS)[None, None, :],
                              padded_start[None, None, :], 0), axis=-1)
    dest = start + rank[:, :TOP_K].astype(jnp.int32)
    block_start = jnp.arange(n_blocks, dtype=jnp.int32) * MOE_ROWS
    block_expert = jnp.minimum(jnp.sum(block_start[:, None] >= padded_end[None, :], axis=1),
                               N_EXPERTS - 1).astype(jnp.int32)
    n_used = (padded_end[-1:] // MOE_ROWS).astype(jnp.int32)
    return dest, block_expert, n_used, padded_end.astype(jnp.int32), n_rows


def _dispatch_kernel(pend_ref, d0_ref, d1_ref, x_ref, xr_hbm, zbuf, sem_z, sem, *, tmb, n_blocks):
    i = pl.program_id(0)

    def zero_copy(e):
        end = pend_ref[e]
        return pltpu.make_async_copy(
            zbuf, xr_hbm.at[pl.ds(pl.multiple_of(end - MOE_ROWS, MOE_ROWS), MOE_ROWS), :], sem_z)

    def nonempty(e):
        return pend_ref[e] > (pend_ref[e - 1] if e > 0 else 0)

    @pl.when(i == 0)
    def _():
        zbuf[...] = jnp.zeros_like(zbuf)
        for e in range(N_EXPERTS):
            @pl.when(nonempty(e))
            def _():
                zero_copy(e).start()
        for e in range(N_EXPERTS):
            @pl.when(nonempty(e))
            def _():
                zero_copy(e).wait()

        def tail_copy(b):
            return pltpu.make_async_copy(
                zbuf, xr_hbm.at[pl.ds(pl.multiple_of(b * MOE_ROWS, MOE_ROWS), MOE_ROWS), :], sem_z)

        def tail_start(b, c):
            tail_copy(b).start()
            return c

        def tail_wait(b, c):
            tail_copy(b).wait()
            return c

        first_unused = pend_ref[N_EXPERTS - 1] // MOE_ROWS
        lax.fori_loop(first_unused, n_blocks, tail_start, 0)
        lax.fori_loop(first_unused, n_blocks, tail_wait, 0)

    def row_copy(r, dst):
        return pltpu.make_async_copy(x_ref.at[pl.ds(r, 1), :], xr_hbm.at[pl.ds(dst, 1), :], sem)

    def issue(g, c):
        for u in range(DMA_UNROLL):
            r = g * DMA_UNROLL + u
            row_copy(r, d0_ref[0, 0, r]).start()
            row_copy(r, d1_ref[0, 0, r]).start()
        return c

    lax.fori_loop(0, tmb // DMA_UNROLL, issue, 0)

    for _ in range(TOP_K):
        pltpu.make_async_copy(x_ref, xr_hbm.at[pl.ds(0, tmb), :], sem).wait()


def _dispatch(x1, dest, padded_end, n_rows, tmb):
    n = x1.shape[0]
    nb = n // tmb
    idx_spec = pl.BlockSpec((1, 1, tmb), lambda i, pe: (i, 0, 0), memory_space=pltpu.SMEM)
    grid_spec = pltpu.PrefetchScalarGridSpec(
        num_scalar_prefetch=1,
        grid=(nb,),
        in_specs=[idx_spec, idx_spec, pl.BlockSpec((tmb, D_MODEL), lambda i, pe: (i, 0))],
        out_specs=pl.BlockSpec(memory_space=pl.ANY),
        scratch_shapes=[pltpu.VMEM((MOE_ROWS, D_MODEL), F32),
                        pltpu.SemaphoreType.DMA, pltpu.SemaphoreType.DMA],
    )
    return pl.pallas_call(
        functools.partial(_dispatch_kernel, tmb=tmb, n_blocks=n_rows // MOE_ROWS),
        grid_spec=grid_spec,
        out_shape=jax.ShapeDtypeStruct((n_rows, D_MODEL), F32),
        compiler_params=_cparams(("arbitrary",)),
    )(padded_end, dest[:, 0].reshape(nb, 1, tmb), dest[:, 1].reshape(nb, 1, tmb), x1)


def _expert_kernel(bexp_ref, nused_ref, x_ref, wg_ref, wu_ref, wd_ref, y_ref, wg_s, wu_s, wd_s):
    i = pl.program_id(0)
    used = i < nused_ref[0]
    new_expert = (i == 0) | (bexp_ref[i] != bexp_ref[jnp.maximum(i - 1, 0)])

    @pl.when(used & new_expert)
    def _():
        wg_s[...] = wg_ref[0].astype(BF16)
        wu_s[...] = wu_ref[0].astype(BF16)
        wd_s[...] = wd_ref[0].astype(BF16)

    @pl.when(used)
    def _():
        xb = x_ref[...].astype(BF16)
        hg = jnp.dot(xb, wg_s[...], preferred_element_type=F32)
        hu = jnp.dot(xb, wu_s[...], preferred_element_type=F32)
        hid = (hg * jax.nn.sigmoid(hg)) * hu
        y_ref[...] = jnp.dot(hid.astype(BF16), wd_s[...], preferred_element_type=F32)

    @pl.when(jnp.logical_not(used))
    def _():
        y_ref[...] = jnp.zeros_like(y_ref)


def _experts(x_rows, block_expert, n_used, wg, wu, wd, layer):
    n_rows = x_rows.shape[0]
    n_blocks = n_rows // MOE_ROWS
    w_map = lambda i, be, nu: (layer, be[i], 0, 0)
    grid_spec = pltpu.PrefetchScalarGridSpec(
        num_scalar_prefetch=2,
        grid=(n_blocks,),
        in_specs=[pl.BlockSpec((MOE_ROWS, D_MODEL), lambda i, be, nu: (jnp.minimum(i, nu[0] - 1), 0)),
                  pl.BlockSpec((None, 1, D_MODEL, D_FF_EXPERT), w_map),
                  pl.BlockSpec((None, 1, D_MODEL, D_FF_EXPERT), w_map),
                  pl.BlockSpec((None, 1, D_FF_EXPERT, D_MODEL), w_map)],
        out_specs=pl.BlockSpec((MOE_ROWS, D_MODEL), lambda i, be, nu: (i, 0)),
        scratch_shapes=[pltpu.VMEM((D_MODEL, D_FF_EXPERT), BF16),
                        pltpu.VMEM((D_MODEL, D_FF_EXPERT), BF16),
                        pltpu.VMEM((D_FF_EXPERT, D_MODEL), BF16)],
    )
    return pl.pallas_call(
        _expert_kernel,
        grid_spec=grid_spec,
        out_shape=jax.ShapeDtypeStruct((n_rows, D_MODEL), F32),
        compiler_params=_cparams(("arbitrary",)),
    )(block_expert, n_used, x_rows, wg, wu, wd)


def _combine_kernel(d0_ref, d1_ref, d0n_ref, d1n_ref, x_ref, route_ref, lng_ref, lnb_ref, y_hbm,
                    o_ref, ybuf, sems, *, alpha, tc, nb):
    i = pl.program_id(0)

    def row_copy(src, slot, choice, r):
        return pltpu.make_async_copy(y_hbm.at[pl.ds(src, 1), :],
                                     ybuf.at[slot, choice, pl.ds(r, 1), :], sems.at[slot])

    def fetch(da_ref, db_ref, slot):
        def issue(g, c):
            for u in range(DMA_UNROLL):
                r = g * DMA_UNROLL + u
                row_copy(da_ref[0, 0, r], slot, 0, r).start()
                row_copy(db_ref[0, 0, r], slot, 1, r).start()
            return c

        lax.fori_loop(0, tc // DMA_UNROLL, issue, 0)

    @pl.when(i == 0)
    def _():
        fetch(d0_ref, d1_ref, 0)

    for slot in range(2):
        @pl.when((i + 1 < nb) & ((i + 1) % 2 == slot))
        def _():
            fetch(d0n_ref, d1n_ref, slot)

    slot = i % 2

    for choice in range(TOP_K):
        pltpu.make_async_copy(y_hbm.at[pl.ds(0, tc), :], ybuf.at[slot, choice], sems.at[slot]).wait()

    r = route_ref[...]
    ffn = r[:, 2:3] * ybuf[slot, 0] + r[:, 3:4] * ybuf[slot, 1]
    o_ref[...] = _layer_norm(alpha * x_ref[...] + ffn, lng_ref[...], lnb_ref[...])


def _combine(x1, y_rows, dest, route, lng, lnb, alpha, tc):
    n = x1.shape[0]
    nb = n // tc
    const2 = lambda i: (0, 0)
    cur = pl.BlockSpec((1, 1, tc), lambda i: (i, 0, 0), memory_space=pltpu.SMEM)
    nxt = pl.BlockSpec((1, 1, tc), lambda i: (jnp.minimum(i + 1, nb - 1), 0, 0), memory_space=pltpu.SMEM)
    d0 = dest[:, 0].reshape(nb, 1, tc)
    d1 = dest[:, 1].reshape(nb, 1, tc)
    return pl.pallas_call(
        functools.partial(_combine_kernel, alpha=alpha, tc=tc, nb=nb),
        grid=(nb,),
        in_specs=[cur, cur, nxt, nxt,
                  pl.BlockSpec((tc, D_MODEL), lambda i: (i, 0)),
                  pl.BlockSpec((tc, LANES), lambda i: (i, 0)),
                  pl.BlockSpec((1, D_MODEL), const2),
                  pl.BlockSpec((1, D_MODEL), const2),
                  pl.BlockSpec(memory_space=pl.ANY)],
        out_specs=pl.BlockSpec((tc, D_MODEL), lambda i: (i, 0)),
        out_shape=jax.ShapeDtypeStruct((n, D_MODEL), F32),
        scratch_shapes=[pltpu.VMEM((2, TOP_K, tc, D_MODEL), F32),
                        pltpu.SemaphoreType.DMA((2,))],
        compiler_params=_cparams(("arbitrary",)),
    )(d0, d1, d0, d1, x1, route, lng, lnb, y_rows)


def _take_cols_kernel(src_ref, a_ref, b_ref, o_ref, *, shift):
    del src_ref
    if shift == 0:
        o_ref[...] = a_ref[...].astype(o_ref.dtype)
    else:
        lane = lax.broadcasted_iota(jnp.int32, a_ref.shape, 1)
        o_ref[...] = jnp.where(lane < LANES - shift,
                               pltpu.roll(a_ref[...], LANES - shift, axis=1),
                               pltpu.roll(b_ref[...], LANES - shift, axis=1)).astype(o_ref.dtype)


def _take_cols(w_all, layer, src_tiles, shift):
    d = w_all.shape[1]
    last = (w_all.shape[2] - 1) // LANES
    src = jnp.asarray(np.asarray(src_tiles, np.int32))
    grid_spec = pltpu.PrefetchScalarGridSpec(
        num_scalar_prefetch=1,
        grid=(len(src_tiles),),
        in_specs=[pl.BlockSpec((None, d, LANES), lambda j, s: (layer, 0, s[j])),
                  pl.BlockSpec((None, d, LANES), lambda j, s: (layer, 0, jnp.minimum(s[j] + 1, last)))],
        out_specs=pl.BlockSpec((d, LANES), lambda j, s: (0, j)),
    )
    return pl.pallas_call(
        functools.partial(_take_cols_kernel, shift=shift),
        grid_spec=grid_spec,
        out_shape=jax.ShapeDtypeStruct((d, len(src_tiles) * LANES), BF16),
        compiler_params=_cparams(("arbitrary",)),
    )(src, w_all, w_all)


def _permute_w_in(w_all, layer):
    per = MIX_WIDTH // LANES
    tiles = lambda t0, nt: list(range(t0, t0 + nt))
    pre = 7 * per
    aligned = tiles(0, per) + tiles(2 * per, 2 * per) + tiles(4 * per, 3 * per)
    gates = tiles(pre + 4 * per, 3 * D_MODEL // LANES)
    c_part = tiles(pre, 4 * per)
    main = jnp.concatenate([_take_cols(w_all, layer, gates, N_HEADS),
                            _take_cols(w_all, layer, aligned, 0),
                            _take_cols(w_all, layer, c_part, N_HEADS)], axis=1)
    fpart = _take_cols(w_all, layer, tiles(per, per) + [pre], 0)
    return main, fpart


def kernel(x, w_in, w_branch, w_out, fox_fgate_bias, hgrn_lb_logits, ln1_g, ln1_b,
           w_router_group, w_router_expert, w_up, w_gate, w_down, ln2_g, ln2_b):
    bsz, seq, d = x.shape
    depth = w_in.shape[0]
    n = bsz * seq
    alpha = float((2 * depth) ** 0.25)
    tm = min(1024, n)
    tb = min(512, seq)
    fox_sub = max(s for s in (1, 2, 4) if seq % (s * FOX_BLOCK) == 0)

    lb_cum = jnp.cumsum(jax.nn.softmax(hgrn_lb_logits.astype(F32), axis=0), axis=0)
    lower_bounds = lb_cum - lb_cum[0]
    tables = _retention_tables(seq)

    h = x.reshape(n, d)
    for layer in range(depth):
        w_main, w_f = _permute_w_in(w_in, layer)
        pb, pf = _project(h, w_main, w_f, tm, 1024)

        lb = lower_bounds[layer][None, :]
        ya = _hgrn2(pb, pf, jnp.log(lb), jnp.log1p(-lb), bsz, seq, tb)

        bias_col = jnp.concatenate([fox_fgate_bias[layer].astype(F32),
                                    jnp.zeros((SUBLANES - N_HEADS,), F32)])[:, None]
        neg_cum = _fox_neg_cum(pf, bias_col, bsz, seq)
        yb = _fox(pb, neg_cum, bsz, seq, fox_sub)

        yc = _retention(pb, tables, bsz, seq, tb)

        w_route = jnp.concatenate(
            [w_router_group[layer], w_router_expert[layer],
             jnp.zeros((d, LANES - N_GROUPS - N_EXPERTS), F32)], axis=1).astype(F32)
        wr_hi = w_route.astype(BF16)
        wr_lo = (w_route - wr_hi.astype(F32)).astype(BF16)
        x1, route = _merge(h, ya, yb, yc, pb, w_branch[layer].astype(BF16), w_out[layer].astype(BF16),
                           ln1_g[layer][None, :], ln1_b[layer][None, :], wr_hi, wr_lo, alpha, min(256, n))

        rank, counts = _rank(route, min(1024, n))
        dest, block_expert, n_used, padded_end, n_rows = _dispatch_plan(route, rank, counts, n)
        x_rows = _dispatch(x1, dest, padded_end, n_rows, min(2048, n))
        y_rows = _experts(x_rows, block_expert, n_used, w_gate, w_up, w_down, layer)
        h = _combine(x1, y_rows, dest, route, ln2_g[layer][None, :], ln2_b[layer][None, :], alpha,
                     min(512, n))
    return h.reshape(bsz, seq, d)
```

```python
import functools
import math

import numpy as np
import jax
import jax.numpy as jnp
from jax import lax
from jax.experimental import pallas as pl
from jax.experimental.pallas import tpu as pltpu

D_MODEL = 1024
HEAD_DIM = 128
MIX_WIDTH = D_MODEL // 2
N_HEADS = MIX_WIDTH // HEAD_DIM
N_GROUPS = 4
EXPERTS_PER_GROUP = 8
N_EXPERTS = N_GROUPS * EXPERTS_PER_GROUP
TOP_K = 2
D_FF_EXPERT = D_MODEL // 2
LN_EPS = 1e-5
HEAD_NORM_EPS = 1e-6
RET_ROPE_BASE = 10000.0

LANES = 128
SUBLANES = 8
HGRN_SUB = 2 * SUBLANES
HGRN_UNROLL = 8
RET_CHUNK = 256
FOX_BLOCK = 512
MOE_ROWS = 512
DMA_UNROLL = 8
VMEM_LIMIT = 48 * 1024 * 1024
LOG2E = 1.4426950408889634

COL_GA, COL_GB, COL_GC = 0, 1024, 2048
COL_AQ, COL_AI, COL_AG = 3072, 3584, 4096
COL_BQ, COL_BK, COL_BV = 4608, 5120, 5632
COL_CQ, COL_CK, COL_CV, COL_CG = 6144, 6656, 7168, 7680
N_MAIN = 8192
N_FGATE = MIX_WIDTH + LANES

F32 = jnp.float32
BF16 = jnp.bfloat16


def _cparams(sem):
    return pltpu.CompilerParams(dimension_semantics=sem, vmem_limit_bytes=VMEM_LIMIT)


def _proj_kernel(x_ref, w_ref, wf_ref, o_ref, of_ref, *, n_main):
    j = pl.program_id(1)
    xb = x_ref[...].astype(BF16)

    @pl.when(j < n_main)
    def _():
        o_ref[...] = jnp.dot(xb, w_ref[...], preferred_element_type=F32).astype(o_ref.dtype)

    @pl.when(j == n_main)
    def _():
        of_ref[...] = jnp.dot(xb, wf_ref[...], preferred_element_type=F32)


def _project(x2d, w, wf, tm, tn):
    n, d = x2d.shape
    c = w.shape[1]
    cf = wf.shape[1]
    n_main = c // tn
    last = n_main - 1
    return pl.pallas_call(
        functools.partial(_proj_kernel, n_main=n_main),
        grid=(n // tm, n_main + 1),
        in_specs=[pl.BlockSpec((tm, d), lambda i, j: (i, 0)),
                  pl.BlockSpec((d, tn), lambda i, j: (0, jnp.minimum(j, last))),
                  pl.BlockSpec((d, cf), lambda i, j: (0, 0))],
        out_specs=[pl.BlockSpec((tm, tn), lambda i, j: (i, jnp.minimum(j, last))),
                   pl.BlockSpec((tm, cf), lambda i, j: (i, 0))],
        out_shape=[jax.ShapeDtypeStruct((n, c), BF16),
                   jax.ShapeDtypeStruct((n, cf), F32)],
        compiler_params=_cparams(("parallel", "arbitrary")),
    )(x2d, w, wf)


def _log_sigmoid(z):
    return jnp.minimum(z, 0.0) - jnp.log1p(jnp.exp(-jnp.abs(z)))


def _hgrn2_kernel(q_ref, i_ref, g_ref, f_ref, loglb_ref, log1m_ref, o_ref, st_ref, rows_ref, *, tb):
    @pl.when(pl.program_id(1) == 0)
    def _():
        st_ref[...] = jnp.zeros_like(st_ref)

    row = lax.broadcasted_iota(jnp.int32, (HGRN_SUB, HEAD_DIM), 0)
    row8 = lax.broadcasted_iota(jnp.int32, (SUBLANES, HEAD_DIM), 0)
    ninf = -jnp.inf
    ones_sq = jnp.ones((HEAD_DIM, HEAD_DIM), BF16)

    def head_step(rows, h, u):
        cols = slice(h * HEAD_DIM, (h + 1) * HEAD_DIM)
        z = f_ref[rows, cols]
        a = loglb_ref[:, cols]
        b = log1m_ref[:, cols] + _log_sigmoid(z)
        log_f = jnp.maximum(a, b) + jnp.log1p(jnp.exp(-jnp.abs(a - b)))
        k = 1.0 - jnp.exp(log_f)
        q = q_ref[rows, cols].astype(F32)
        v = i_ref[rows, cols].astype(F32)
        cum = log_f * LOG2E
        for s in (1, 2, 4, 8):
            cum = cum + jnp.where(row >= s, pltpu.roll(cum, s, axis=0), 0.0)
        q_t, q_b = q[:SUBLANES], q[SUBLANES:]
        c_t, c_b = cum[:SUBLANES], cum[SUBLANES:]
        rows_ref[u, h, 0] = cum
        rows_ref[u, h, 1] = k
        rows_ref[u, h, 2] = v
        w_top, w_bot = [], []
        for s in range(HGRN_SUB):
            cs, ks = rows_ref[u, h, 0, s:s + 1, :], rows_ref[u, h, 1, s:s + 1, :]
            if s < SUBLANES:
                w_bot.append(q_b * ks * jnp.exp2(c_b - cs))
                e_t = c_t - cs
                if s > 0:
                    e_t = jnp.where(row8 >= s, e_t, ninf)
                w_top.append(q_t * ks * jnp.exp2(e_t))
            else:
                e_b = c_b - cs
                if s > SUBLANES:
                    e_b = jnp.where(row8 >= s - SUBLANES, e_b, ninf)
                w_bot.append(q_b * ks * jnp.exp2(e_b))
        scores = jnp.dot(jnp.concatenate(w_top + w_bot, axis=0).astype(BF16), ones_sq,
                         preferred_element_type=F32)
        o_t = jnp.zeros((SUBLANES, HEAD_DIM), F32)
        o_b = jnp.zeros((SUBLANES, HEAD_DIM), F32)
        for s in range(HGRN_SUB):
            vs = rows_ref[u, h, 2, s:s + 1, :]
            if s < SUBLANES:
                o_t = o_t + scores[s * SUBLANES:(s + 1) * SUBLANES] * vs
            b0 = (SUBLANES + s) * SUBLANES
            o_b = o_b + scores[b0:b0 + SUBLANES] * vs
        o = jnp.concatenate([o_t, o_b], axis=0)
        st = st_ref[h]
        qd = (q * jnp.exp2(cum)).astype(BF16)
        o = o + lax.dot_general(qd, st.astype(BF16), (((1,), (1,)), ((), ())),
                                preferred_element_type=F32)
        last = cum[HGRN_SUB - 1:HGRN_SUB, :]
        kd = (k * jnp.exp2(last - cum)).astype(BF16)
        upd = lax.dot_general(v.astype(BF16), kd, (((0,), (0,)), ((), ())),
                              preferred_element_type=F32)
        st_ref[h] = st * jnp.exp2(last) + upd
        y = o * lax.rsqrt(jnp.mean(o * o, axis=1, keepdims=True) + HEAD_NORM_EPS)
        y = y * jax.nn.sigmoid(g_ref[rows, cols].astype(F32))
        o_ref[rows, cols] = y.astype(o_ref.dtype)

    def step(j, carry):
        for u in range(HGRN_UNROLL):
            r0 = pl.multiple_of((j * HGRN_UNROLL + u) * HGRN_SUB, HGRN_SUB)
            for h in range(N_HEADS):
                head_step(pl.ds(r0, HGRN_SUB), h, u)
        return carry

    lax.fori_loop(0, tb // (HGRN_SUB * HGRN_UNROLL), step, 0)


def _hgrn2(pb, pf, log_lb, log1m_lb, bsz, seq, tb):
    n = bsz * seq
    nt = seq // tb
    cb = lambda c: c // MIX_WIDTH
    row_map = lambda c: (lambda b, t: (b * nt + t, c))
    return pl.pallas_call(
        functools.partial(_hgrn2_kernel, tb=tb),
        grid=(bsz, nt),
        in_specs=[pl.BlockSpec((tb, MIX_WIDTH), row_map(cb(COL_AQ))),
                  pl.BlockSpec((tb, MIX_WIDTH), row_map(cb(COL_AI))),
                  pl.BlockSpec((tb, MIX_WIDTH), row_map(cb(COL_AG))),
                  pl.BlockSpec((tb, MIX_WIDTH), row_map(0)),
                  pl.BlockSpec((1, MIX_WIDTH), lambda b, t: (0, 0)),
                  pl.BlockSpec((1, MIX_WIDTH), lambda b, t: (0, 0))],
        out_specs=pl.BlockSpec((tb, MIX_WIDTH), row_map(0)),
        out_shape=jax.ShapeDtypeStruct((n, MIX_WIDTH), BF16),
        scratch_shapes=[pltpu.VMEM((N_HEADS, HEAD_DIM, HEAD_DIM), F32),
                        pltpu.VMEM((HGRN_UNROLL, N_HEADS, 3, HGRN_SUB, HEAD_DIM), F32)],
        compiler_params=_cparams(("parallel", "arbitrary")),
    )(pb, pb, pb, pf, log_lb, log1m_lb)


def _retention_kernel(q_ref, k_ref, v_ref, g_ref, cos_ref, sin_ref, dmask_ref, qdec_ref, kdec_ref,
                      o_ref, st_ref, *, tb, chunk_decay):
    @pl.when(pl.program_id(1) == 0)
    def _():
        st_ref[...] = jnp.zeros_like(st_ref)

    def step(j, carry):
        r0 = pl.multiple_of(j * RET_CHUNK, RET_CHUNK)
        rows = pl.ds(r0, RET_CHUNK)
        cos = cos_ref[rows, :]
        sin = sin_ref[rows, :]
        for h in range(N_HEADS):
            cols = slice(h * HEAD_DIM, (h + 1) * HEAD_DIM)
            q = q_ref[rows, cols].astype(F32)
            k = k_ref[rows, cols].astype(F32)
            q = q * cos + pltpu.roll(q, HEAD_DIM // 2, axis=1) * sin
            k = (k * cos + pltpu.roll(k, HEAD_DIM // 2, axis=1) * sin) * (HEAD_DIM ** -0.5)
            v = v_ref[rows, cols]
            inner = lax.dot_general(q.astype(BF16), k.astype(BF16), (((1,), (1,)), ((), ())),
                                    preferred_element_type=F32) * dmask_ref[h]
            st = st_ref[h]
            o = (jnp.dot(inner.astype(BF16), v, preferred_element_type=F32)
                 + jnp.dot((q * qdec_ref[h]).astype(BF16), st.astype(BF16),
                           preferred_element_type=F32))
            upd = lax.dot_general((k * kdec_ref[h]).astype(BF16), v, (((0,), (0,)), ((), ())),
                                  preferred_element_type=F32)
            st_ref[h] = st * chunk_decay[h] + upd
            c = o - jnp.mean(o, axis=1, keepdims=True)
            y = c * lax.rsqrt(jnp.mean(c * c, axis=1, keepdims=True) + HEAD_NORM_EPS)
            g = g_ref[rows, cols].astype(F32)
            o_ref[rows, cols] = (y * (g * jax.nn.sigmoid(g))).astype(o_ref.dtype)
        return carry

    lax.fori_loop(0, tb // RET_CHUNK, step, 0)


def _retention_tables(seq):
    half = HEAD_DIM // 2
    inv = 1.0 / (RET_ROPE_BASE ** jnp.linspace(0.0, 1.0, half, dtype=F32))
    ang = jnp.arange(seq, dtype=F32)[:, None] * inv[None, :]
    cos = jnp.cos(ang)
    sin = jnp.sin(ang)
    cos_t = jnp.concatenate([cos, cos], axis=-1)
    sin_t = jnp.concatenate([-sin, sin], axis=-1)
    log_gamma = jnp.log(1.0 - jnp.power(2.0, -5.0 - jnp.arange(N_HEADS, dtype=F32)))
    idx = jnp.arange(RET_CHUNK, dtype=F32)
    rel = idx[:, None] - idx[None, :]
    dmask = jnp.where(rel >= 0, jnp.exp(log_gamma[:, None, None] * jnp.maximum(rel, 0.0)), 0.0)
    ones = jnp.ones((1, 1, HEAD_DIM), F32)
    qdec = jnp.exp(log_gamma[:, None] * (idx + 1.0))[..., None] * ones
    kdec = jnp.exp(log_gamma[:, None] * (RET_CHUNK - 1.0 - idx))[..., None] * ones
    return cos_t, sin_t, dmask, qdec, kdec


def _retention(pb, tables, bsz, seq, tb):
    n = bsz * seq
    nt = seq // tb
    cos_t, sin_t, dmask, qdec, kdec = tables
    chunk_decay = tuple(float((1.0 - 2.0 ** (-5.0 - h)) ** RET_CHUNK) for h in range(N_HEADS))
    cb = lambda c: c // MIX_WIDTH
    row_map = lambda c: (lambda b, t: (b * nt + t, c))
    const3 = lambda b, t: (0, 0, 0)
    return pl.pallas_call(
        functools.partial(_retention_kernel, tb=tb, chunk_decay=chunk_decay),
        grid=(bsz, nt),
        in_specs=[pl.BlockSpec((tb, MIX_WIDTH), row_map(cb(COL_CQ))),
                  pl.BlockSpec((tb, MIX_WIDTH), row_map(cb(COL_CK))),
                  pl.BlockSpec((tb, MIX_WIDTH), row_map(cb(COL_CV))),
                  pl.BlockSpec((tb, MIX_WIDTH), row_map(cb(COL_CG))),
                  pl.BlockSpec((tb, HEAD_DIM), lambda b, t: (t, 0)),
                  pl.BlockSpec((tb, HEAD_DIM), lambda b, t: (t, 0)),
                  pl.BlockSpec((N_HEADS, RET_CHUNK, RET_CHUNK), const3),
                  pl.BlockSpec((N_HEADS, RET_CHUNK, HEAD_DIM), const3),
                  pl.BlockSpec((N_HEADS, RET_CHUNK, HEAD_DIM), const3)],
        out_specs=pl.BlockSpec((tb, MIX_WIDTH), row_map(0)),
        out_shape=jax.ShapeDtypeStruct((n, MIX_WIDTH), BF16),
        scratch_shapes=[pltpu.VMEM((N_HEADS, HEAD_DIM, HEAD_DIM), F32)],
        compiler_params=_cparams(("parallel", "arbitrary")),
    )(pb, pb, pb, pb, cos_t, sin_t, dmask, qdec, kdec)


def _fox_cum_kernel(z_ref, bias_ref, o_ref):
    z = z_ref[...]
    hi = z.astype(BF16)
    rem = z - hi.astype(F32)
    mid = rem.astype(BF16)
    lo = (rem - mid.astype(F32)).astype(BF16)
    sel = jnp.where(lax.broadcasted_iota(jnp.int32, (SUBLANES, LANES), 0)
                    == lax.broadcasted_iota(jnp.int32, (SUBLANES, LANES), 1), 1.0, 0.0).astype(BF16)
    nt = (((1,), (1,)), ((), ()))
    zt = (lax.dot_general(sel, hi, nt, preferred_element_type=F32)
          + lax.dot_general(sel, mid, nt, preferred_element_type=F32)
          + lax.dot_general(sel, lo, nt, preferred_element_type=F32))
    x = _log_sigmoid(zt + bias_ref[...])
    lane = lax.broadcasted_iota(jnp.int32, x.shape, 1)
    s = 1
    while s < x.shape[1]:
        x = x + jnp.where(lane >= s, pltpu.roll(x, s, axis=1), 0.0)
        s *= 2
    o_ref[...] = x * (-LOG2E)


def _fox_neg_cum(pf, bias_col, bsz, seq):
    return pl.pallas_call(
        _fox_cum_kernel,
        grid=(bsz,),
        in_specs=[pl.BlockSpec((seq, LANES), lambda b: (b, MIX_WIDTH // LANES)),
                  pl.BlockSpec((SUBLANES, 1), lambda b: (0, 0))],
        out_specs=pl.BlockSpec((None, SUBLANES, seq), lambda b: (b, 0, 0)),
        out_shape=jax.ShapeDtypeStruct((bsz, SUBLANES, seq), F32),
        compiler_params=_cparams(("parallel",)),
    )(pf, bias_col)


def _fox_kernel(q_ref, k_ref, v_ref, nck_ref, o_ref, m_ref, acc_ref, *, nsub, scale):
    i = pl.program_id(2)
    tk = FOX_BLOCK
    m_ref[...] = jnp.full_like(m_ref, -jnp.inf)
    acc_ref[...] = jnp.zeros_like(acc_ref)
    qs = [(q_ref[a * tk:(a + 1) * tk, :].astype(F32) * (scale * LOG2E)).astype(BF16)
          for a in range(nsub)]
    ones = jnp.ones((tk, HEAD_DIM), BF16)
    causal = (lax.broadcasted_iota(jnp.int32, (tk, tk), 1)
              <= lax.broadcasted_iota(jnp.int32, (tk, tk), 0))

    def attend(a, kblk, masked):
        rows = pl.ds(pl.multiple_of(kblk * tk, tk), tk)
        s = lax.dot_general(qs[a], k_ref[rows, :], (((1,), (1,)), ((), ())),
                            preferred_element_type=F32) + nck_ref[kblk]
        if masked:
            s = jnp.where(causal, s, -jnp.inf)
        m_prev = m_ref[a]
        m_new = jnp.maximum(m_prev, jnp.max(s, axis=1, keepdims=True))
        p = jnp.exp2(s - jnp.concatenate([m_new] * (tk // LANES), axis=1)).astype(BF16)
        v_aug = jnp.concatenate([v_ref[rows, :], ones], axis=1)
        alpha = jnp.exp2(m_prev - m_new)
        acc_ref[a] = jnp.concatenate([alpha, alpha], axis=1) * acc_ref[a] + jnp.dot(
            p, v_aug, preferred_element_type=F32)
        m_ref[a] = m_new

    def below_diagonal(j, c):
        for a in range(nsub):
            attend(a, j, False)
        return c

    lax.fori_loop(0, i * nsub, below_diagonal, 0)
    for d in range(nsub):
        for a in range(d, nsub):
            attend(a, i * nsub + d, a == d)
    for a in range(nsub):
        acc = acc_ref[a]
        o_ref[a * tk:(a + 1) * tk, :] = (acc[:, :HEAD_DIM] / acc[:, HEAD_DIM:]).astype(o_ref.dtype)


def _fox(pb, neg_cum, bsz, seq, nsub):
    n = bsz * seq
    tq = nsub * FOX_BLOCK
    nq = seq // tq
    nkb = seq // FOX_BLOCK
    cq, ck, cv = COL_BQ // HEAD_DIM, COL_BK // HEAD_DIM, COL_BV // HEAD_DIM
    return pl.pallas_call(
        functools.partial(_fox_kernel, nsub=nsub, scale=HEAD_DIM ** -0.5),
        grid=(bsz, N_HEADS, nq),
        in_specs=[pl.BlockSpec((tq, HEAD_DIM), lambda b, h, i: (b * nq + i, cq + h)),
                  pl.BlockSpec((seq, HEAD_DIM), lambda b, h, i: (b, ck + h)),
                  pl.BlockSpec((seq, HEAD_DIM), lambda b, h, i: (b, cv + h)),
                  pl.BlockSpec((None, None, nkb, 1, FOX_BLOCK), lambda b, h, i: (b, h, 0, 0, 0))],
        out_specs=pl.BlockSpec((tq, HEAD_DIM), lambda b, h, i: (b * nq + i, h)),
        out_shape=jax.ShapeDtypeStruct((n, MIX_WIDTH), BF16),
        scratch_shapes=[pltpu.VMEM((nsub, FOX_BLOCK, LANES), F32),
                        pltpu.VMEM((nsub, FOX_BLOCK, 2 * HEAD_DIM), F32)],
        compiler_params=_cparams(("parallel", "parallel", "arbitrary")),
    )(pb, pb, pb, neg_cum.reshape(bsz, SUBLANES, nkb, 1, FOX_BLOCK))


def _layer_norm(x, g, b):
    mu = jnp.mean(x, axis=1, keepdims=True)
    c = x - mu
    var = jnp.mean(c * c, axis=1, keepdims=True)
    return c * lax.rsqrt(var + LN_EPS) * g + b


def _route(logits):
    lane = lax.broadcasted_iota(jnp.int32, logits.shape, 1)
    lane_f = lane.astype(F32)
    ninf = -jnp.inf
    big = float(LANES)
    gl = jnp.where(lane < N_GROUPS, logits, ninf)
    gmax = jnp.max(gl, axis=1, keepdims=True)
    gidx = jnp.min(jnp.where(gl == gmax, lane_f, big), axis=1, keepdims=True)
    gprob = 1.0 / jnp.sum(jnp.exp(gl - gmax), axis=1, keepdims=True)
    e_group = ((lane - N_GROUPS) // EXPERTS_PER_GROUP).astype(F32)
    in_grp = (lane >= N_GROUPS) & (lane < N_GROUPS + N_EXPERTS) & (e_group == gidx)
    el = jnp.where(in_grp, logits, ninf)
    t1 = jnp.max(el, axis=1, keepdims=True)
    i1 = jnp.min(jnp.where(el == t1, lane_f, big), axis=1, keepdims=True)
    el2 = jnp.where(lane_f == i1, ninf, el)
    t2 = jnp.max(el2, axis=1, keepdims=True)
    i2 = jnp.min(jnp.where(el2 == t2, lane_f, big), axis=1, keepdims=True)
    d = jnp.exp(t2 - t1)
    g1 = gprob / (1.0 + d)
    g2 = gprob * d / (1.0 + d)
    out = jnp.where(lane == 0, i1 - N_GROUPS, 0.0)
    out = jnp.where(lane == 1, i2 - N_GROUPS, out)
    out = jnp.where(lane == 2, g1, out)
    out = jnp.where(lane == 3, g2, out)
    return out


def _merge_kernel(x_ref, ya_ref, yb_ref, yc_ref, ga_ref, gb_ref, gc_ref, wb_ref, wo_ref,
                  lng_ref, lnb_ref, wrh_ref, wrl_ref, x1_ref, logit_ref, *, alpha):
    def branch(y_ref, g_ref, idx):
        return jax.nn.sigmoid(g_ref[...].astype(F32)) * jnp.dot(
            y_ref[...], wb_ref[idx], preferred_element_type=F32)

    merged = branch(ya_ref, ga_ref, 0) + branch(yb_ref, gb_ref, 1) + branch(yc_ref, gc_ref, 2)
    mix = jnp.dot(merged.astype(BF16), wo_ref[...], preferred_element_type=F32)
    x1 = _layer_norm(alpha * x_ref[...] + mix, lng_ref[...], lnb_ref[...])
    x1_ref[...] = x1
    x_hi = x1.astype(BF16)
    x_lo = (x1 - x_hi.astype(F32)).astype(BF16)
    logits = (jnp.dot(x_hi, wrh_ref[...], preferred_element_type=F32)
              + jnp.dot(x_lo, wrh_ref[...], preferred_element_type=F32)
              + jnp.dot(x_hi, wrl_ref[...], preferred_element_type=F32))
    logit_ref[...] = logits


def _merge(x2d, ya, yb, yc, pb, wb, wo, lng, lnb, wr_hi, wr_lo, alpha, tm):
    n = x2d.shape[0]
    row = lambda c: (lambda i: (i, c))
    const2 = lambda i: (0, 0)
    return pl.pallas_call(
        functools.partial(_merge_kernel, alpha=alpha),
        grid=(n // tm,),
        in_specs=[pl.BlockSpec((tm, D_MODEL), row(0)),
                  pl.BlockSpec((tm, MIX_WIDTH), row(0)),
                  pl.BlockSpec((tm, MIX_WIDTH), row(0)),
                  pl.BlockSpec((tm, MIX_WIDTH), row(0)),
                  pl.BlockSpec((tm, D_MODEL), row(COL_GA // D_MODEL)),
                  pl.BlockSpec((tm, D_MODEL), row(COL_GB // D_MODEL)),
                  pl.BlockSpec((tm, D_MODEL), row(COL_GC // D_MODEL)),
                  pl.BlockSpec((3, MIX_WIDTH, D_MODEL), lambda i: (0, 0, 0)),
                  pl.BlockSpec((D_MODEL, D_MODEL), const2),
                  pl.BlockSpec((1, D_MODEL), const2),
                  pl.BlockSpec((1, D_MODEL), const2),
                  pl.BlockSpec((D_MODEL, LANES), const2),
                  pl.BlockSpec((D_MODEL, LANES), const2)],
        out_specs=[pl.BlockSpec((tm, D_MODEL), row(0)),
                   pl.BlockSpec((tm, LANES), row(0))],
        out_shape=[jax.ShapeDtypeStruct((n, D_MODEL), F32),
                   jax.ShapeDtypeStruct((n, LANES), F32)],
        compiler_params=_cparams(("parallel",)),
    )(x2d, ya, yb, yc, pb, pb, pb, wb, wo, lng, lnb, wr_hi, wr_lo)


def _rank_kernel(logit_ref, route_ref, rank_ref, cnt_ref, carry_ref, *, tr):
    @pl.when(pl.program_id(0) == 0)
    def _():
        carry_ref[...] = jnp.zeros_like(carry_ref)

    r = _route(logit_ref[...])
    route_ref[...] = r
    lane = lax.broadcasted_iota(jnp.int32, r.shape, 1)
    lane_f = lane.astype(F32)
    oh1 = lane_f == r[:, 0:1]
    oh2 = lane_f == r[:, 1:2]
    oh = jnp.where(oh1, 1.0, jnp.where(oh2, 1.0, 0.0))
    earlier = jnp.where(lax.broadcasted_iota(jnp.int32, (tr, tr), 1)
                        < lax.broadcasted_iota(jnp.int32, (tr, tr), 0), 1.0, 0.0).astype(BF16)
    before = jnp.dot(earlier, oh.astype(BF16), preferred_element_type=F32) + carry_ref[...]
    rank1 = jnp.sum(jnp.where(oh1, before, 0.0), axis=1, keepdims=True)
    rank2 = jnp.sum(jnp.where(oh2, before, 0.0), axis=1, keepdims=True)
    rank_ref[...] = jnp.where(lane == 0, rank1, jnp.where(lane == 1, rank2, 0.0))
    total = carry_ref[...] + jnp.sum(oh, axis=0, keepdims=True)
    carry_ref[...] = total
    cnt_ref[...] = total


def _rank(logits, tr):
    n = logits.shape[0]
    return pl.pallas_call(
        functools.partial(_rank_kernel, tr=tr),
        grid=(n // tr,),
        in_specs=[pl.BlockSpec((tr, LANES), lambda i: (i, 0))],
        out_specs=[pl.BlockSpec((tr, LANES), lambda i: (i, 0)),
                   pl.BlockSpec((tr, LANES), lambda i: (i, 0)),
                   pl.BlockSpec((1, LANES), lambda i: (0, 0))],
        out_shape=[jax.ShapeDtypeStruct((n, LANES), F32),
                   jax.ShapeDtypeStruct((n, LANES), F32),
                   jax.ShapeDtypeStruct((1, LANES), F32)],
        scratch_shapes=[pltpu.VMEM((1, LANES), F32)],
        compiler_params=_cparams(("arbitrary",)),
    )(logits)


def _dispatch_plan(route, rank, counts, n_tok):
    n_rows = n_tok * TOP_K + N_EXPERTS * MOE_ROWS
    n_blocks = n_rows // MOE_ROWS
    cnt = counts[0, :N_EXPERTS].astype(jnp.int32)
    padded = (cnt + MOE_ROWS - 1) // MOE_ROWS * MOE_ROWS
    padded_end = jnp.cumsum(padded)
    padded_start = padded_end - padded
    expert = route[:, :TOP_K].astype(jnp.int32)
    start = jnp.sum(jnp.where(expert[:, :, None] == jnp.arange(N_EXPERTS)[None, None, :],
                              padded_start[None, None, :], 0), axis=-1)
    dest = start + rank[:, :TOP_K].astype(jnp.int32)
    block_start = jnp.arange(n_blocks, dtype=jnp.int32) * MOE_ROWS
    block_expert = jnp.minimum(jnp.sum(block_start[:, None] >= padded_end[None, :], axis=1),
                               N_EXPERTS - 1).astype(jnp.int32)
    n_used = (padded_end[-1:] // MOE_ROWS).astype(jnp.int32)
    return dest, block_expert, n_used, padded_end.astype(jnp.int32), n_rows


def _dispatch_kernel(pend_ref, d0_ref, d1_ref, x_ref, xr_hbm, zbuf, sem_z, sem, *, tmb, n_blocks):
    i = pl.program_id(0)

    def zero_copy(e):
        end = pend_ref[e]
        return pltpu.make_async_copy(
            zbuf, xr_hbm.at[pl.ds(pl.multiple_of(end - MOE_ROWS, MOE_ROWS), MOE_ROWS), :], sem_z)

    def nonempty(e):
        return pend_ref[e] > (pend_ref[e - 1] if e > 0 else 0)

    @pl.when(i == 0)
    def _():
        zbuf[...] = jnp.zeros_like(zbuf)
        for e in range(N_EXPERTS):
            @pl.when(nonempty(e))
            def _():
                zero_copy(e).start()
        for e in range(N_EXPERTS):
            @pl.when(nonempty(e))
            def _():
                zero_copy(e).wait()

        def tail_copy(b):
            return pltpu.make_async_copy(
                zbuf, xr_hbm.at[pl.ds(pl.multiple_of(b * MOE_ROWS, MOE_ROWS), MOE_ROWS), :], sem_z)

        def tail_start(b, c):
            tail_copy(b).start()
            return c

        def tail_wait(b, c):
            tail_copy(b).wait()
            return c

        first_unused = pend_ref[N_EXPERTS - 1] // MOE_ROWS
        lax.fori_loop(first_unused, n_blocks, tail_start, 0)
        lax.fori_loop(first_unused, n_blocks, tail_wait, 0)

    def row_copy(r, dst):
        return pltpu.make_async_copy(x_ref.at[pl.ds(r, 1), :], xr_hbm.at[pl.ds(dst, 1), :], sem)

    def issue(g, c):
        for u in range(DMA_UNROLL):
            r = g * DMA_UNROLL + u
            row_copy(r, d0_ref[0, 0, r]).start()
            row_copy(r, d1_ref[0, 0, r]).start()
        return c

    lax.fori_loop(0, tmb // DMA_UNROLL, issue, 0)

    for _ in range(TOP_K):
        pltpu.make_async_copy(x_ref, xr_hbm.at[pl.ds(0, tmb), :], sem).wait()


def _dispatch(x1, dest, padded_end, n_rows, tmb):
    n = x1.shape[0]
    nb = n // tmb
    idx_spec = pl.BlockSpec((1, 1, tmb), lambda i, pe: (i, 0, 0), memory_space=pltpu.SMEM)
    grid_spec = pltpu.PrefetchScalarGridSpec(
        num_scalar_prefetch=1,
        grid=(nb,),
        in_specs=[idx_spec, idx_spec, pl.BlockSpec((tmb, D_MODEL), lambda i, pe: (i, 0))],
        out_specs=pl.BlockSpec(memory_space=pl.ANY),
        scratch_shapes=[pltpu.VMEM((MOE_ROWS, D_MODEL), F32),
                        pltpu.SemaphoreType.DMA, pltpu.SemaphoreType.DMA],
    )
    return pl.pallas_call(
        functools.partial(_dispatch_kernel, tmb=tmb, n_blocks=n_rows // MOE_ROWS),
        grid_spec=grid_spec,
        out_shape=jax.ShapeDtypeStruct((n_rows, D_MODEL), F32),
        compiler_params=_cparams(("arbitrary",)),
    )(padded_end, dest[:, 0].reshape(nb, 1, tmb), dest[:, 1].reshape(nb, 1, tmb), x1)


def _expert_kernel(bexp_ref, nused_ref, x_ref, wg_ref, wu_ref, wd_ref, y_ref, wg_s, wu_s, wd_s):
    i = pl.program_id(0)
    used = i < nused_ref[0]
    new_expert = (i == 0) | (bexp_ref[i] != bexp_ref[jnp.maximum(i - 1, 0)])

    @pl.when(used & new_expert)
    def _():
        wg_s[...] = wg_ref[0].astype(BF16)
        wu_s[...] = wu_ref[0].astype(BF16)
        wd_s[...] = wd_ref[0].astype(BF16)

    @pl.when(used)
    def _():
        xb = x_ref[...].astype(BF16)
        hg = jnp.dot(xb, wg_s[...], preferred_element_type=F32)
        hu = jnp.dot(xb, wu_s[...], preferred_element_type=F32)
        hid = (hg * jax.nn.sigmoid(hg)) * hu
        y_ref[...] = jnp.dot(hid.astype(BF16), wd_s[...], preferred_element_type=F32)

    @pl.when(jnp.logical_not(used))
    def _():
        y_ref[...] = jnp.zeros_like(y_ref)


def _experts(x_rows, block_expert, n_used, wg, wu, wd, layer):
    n_rows = x_rows.shape[0]
    n_blocks = n_rows // MOE_ROWS
    w_map = lambda i, be, nu: (layer, be[i], 0, 0)
    grid_spec = pltpu.PrefetchScalarGridSpec(
        num_scalar_prefetch=2,
        grid=(n_blocks,),
        in_specs=[pl.BlockSpec((MOE_ROWS, D_MODEL), lambda i, be, nu: (jnp.minimum(i, nu[0] - 1), 0)),
                  pl.BlockSpec((None, 1, D_MODEL, D_FF_EXPERT), w_map),
                  pl.BlockSpec((None, 1, D_MODEL, D_FF_EXPERT), w_map),
                  pl.BlockSpec((None, 1, D_FF_EXPERT, D_MODEL), w_map)],
        out_specs=pl.BlockSpec((MOE_ROWS, D_MODEL), lambda i, be, nu: (i, 0)),
        scratch_shapes=[pltpu.VMEM((D_MODEL, D_FF_EXPERT), BF16),
                        pltpu.VMEM((D_MODEL, D_FF_EXPERT), BF16),
                        pltpu.VMEM((D_FF_EXPERT, D_MODEL), BF16)],
    )
    return pl.pallas_call(
        _expert_kernel,
        grid_spec=grid_spec,
        out_shape=jax.ShapeDtypeStruct((n_rows, D_MODEL), F32),
        compiler_params=_cparams(("arbitrary",)),
    )(block_expert, n_used, x_rows, wg, wu, wd)


def _combine_kernel(d0_ref, d1_ref, d0n_ref, d1n_ref, x_ref, route_ref, lng_ref, lnb_ref, y_hbm,
                    o_ref, ybuf, sems, *, alpha, tc, nb):
    i = pl.program_id(0)

    def row_copy(src, slot, choice, r):
        return pltpu.make_async_copy(y_hbm.at[pl.ds(src, 1), :],
                                     ybuf.at[slot, choice, pl.ds(r, 1), :], sems.at[slot])

    def fetch(da_ref, db_ref, slot):
        def issue(g, c):
            for u in range(DMA_UNROLL):
                r = g * DMA_UNROLL + u
                row_copy(da_ref[0, 0, r], slot, 0, r).start()
                row_copy(db_ref[0, 0, r], slot, 1, r).start()
            return c

        lax.fori_loop(0, tc // DMA_UNROLL, issue, 0)

    @pl.when(i == 0)
    def _():
        fetch(d0_ref, d1_ref, 0)

    for slot in range(2):
        @pl.when((i + 1 < nb) & ((i + 1) % 2 == slot))
        def _():
            fetch(d0n_ref, d1n_ref, slot)

    slot = i % 2

    for choice in range(TOP_K):
        pltpu.make_async_copy(y_hbm.at[pl.ds(0, tc), :], ybuf.at[slot, choice], sems.at[slot]).wait()

    r = route_ref[...]
    ffn = r[:, 2:3] * ybuf[slot, 0] + r[:, 3:4] * ybuf[slot, 1]
    o_ref[...] = _layer_norm(alpha * x_ref[...] + ffn, lng_ref[...], lnb_ref[...])


def _combine(x1, y_rows, dest, route, lng, lnb, alpha, tc):
    n = x1.shape[0]
    nb = n // tc
    const2 = lambda i: (0, 0)
    cur = pl.BlockSpec((1, 1, tc), lambda i: (i, 0, 0), memory_space=pltpu.SMEM)
    nxt = pl.BlockSpec((1, 1, tc), lambda i: (jnp.minimum(i + 1, nb - 1), 0, 0), memory_space=pltpu.SMEM)
    d0 = dest[:, 0].reshape(nb, 1, tc)
    d1 = dest[:, 1].reshape(nb, 1, tc)
    return pl.pallas_call(
        functools.partial(_combine_kernel, alpha=alpha, tc=tc, nb=nb),
        grid=(nb,),
        in_specs=[cur, cur, nxt, nxt,
                  pl.BlockSpec((tc, D_MODEL), lambda i: (i, 0)),
                  pl.BlockSpec((tc, LANES), lambda i: (i, 0)),
                  pl.BlockSpec((1, D_MODEL), const2),
                  pl.BlockSpec((1, D_MODEL), const2),
                  pl.BlockSpec(memory_space=pl.ANY)],
        out_specs=pl.BlockSpec((tc, D_MODEL), lambda i: (i, 0)),
        out_shape=jax.ShapeDtypeStruct((n, D_MODEL), F32),
        scratch_shapes=[pltpu.VMEM((2, TOP_K, tc, D_MODEL), F32),
                        pltpu.SemaphoreType.DMA((2,))],
        compiler_params=_cparams(("arbitrary",)),
    )(d0, d1, d0, d1, x1, route, lng, lnb, y_rows)


def _take_cols_kernel(src_ref, a_ref, b_ref, o_ref, *, shift):
    del src_ref
    if shift == 0:
        o_ref[...] = a_ref[...].astype(o_ref.dtype)
    else:
        lane = lax.broadcasted_iota(jnp.int32, a_ref.shape, 1)
        o_ref[...] = jnp.where(lane < LANES - shift,
                               pltpu.roll(a_ref[...], LANES - shift, axis=1),
                               pltpu.roll(b_ref[...], LANES - shift, axis=1)).astype(o_ref.dtype)


def _take_cols(w_all, layer, src_tiles, shift):
    d = w_all.shape[1]
    last = (w_all.shape[2] - 1) // LANES
    src = jnp.asarray(np.asarray(src_tiles, np.int32))
    grid_spec = pltpu.PrefetchScalarGridSpec(
        num_scalar_prefetch=1,
        grid=(len(src_tiles),),
        in_specs=[pl.BlockSpec((None, d, LANES), lambda j, s: (layer, 0, s[j])),
                  pl.BlockSpec((None, d, LANES), lambda j, s: (layer, 0, jnp.minimum(s[j] + 1, last)))],
        out_specs=pl.BlockSpec((d, LANES), lambda j, s: (0, j)),
    )
    return pl.pallas_call(
        functools.partial(_take_cols_kernel, shift=shift),
        grid_spec=grid_spec,
        out_shape=jax.ShapeDtypeStruct((d, len(src_tiles) * LANES), BF16),
        compiler_params=_cparams(("arbitrary",)),
    )(src, w_all, w_all)


def _permute_w_in(w_all, layer):
    per = MIX_WIDTH // LANES
    tiles = lambda t0, nt: list(range(t0, t0 + nt))
    pre = 7 * per
    aligned = tiles(0, per) + tiles(2 * per, 2 * per) + tiles(4 * per, 3 * per)
    gates = tiles(pre + 4 * per, 3 * D_MODEL // LANES)
    c_part = tiles(pre, 4 * per)
    main = jnp.concatenate([_take_cols(w_all, layer, gates, N_HEADS),
                            _take_cols(w_all, layer, aligned, 0),
                            _take_cols(w_all, layer, c_part, N_HEADS)], axis=1)
    fpart = _take_cols(w_all, layer, tiles(per, per) + [pre], 0)
    return main, fpart


def kernel(x, w_in, w_branch, w_out, fox_fgate_bias, hgrn_lb_logits, ln1_g, ln1_b,
           w_router_group, w_router_expert, w_up, w_gate, w_down, ln2_g, ln2_b):
    bsz, seq, d = x.shape
    depth = w_in.shape[0]
    n = bsz * seq
    alpha = float((2 * depth) ** 0.25)
    tm = min(1024, n)
    tb = min(512, seq)
    fox_sub = max(s for s in (1, 2, 4, 8) if seq % (s * FOX_BLOCK) == 0)

    lb_cum = jnp.cumsum(jax.nn.softmax(hgrn_lb_logits.astype(F32), axis=0), axis=0)
    lower_bounds = lb_cum - lb_cum[0]
    tables = _retention_tables(seq)

    h = x.reshape(n, d)
    for layer in range(depth):
        w_main, w_f = _permute_w_in(w_in, layer)
        pb, pf = _project(h, w_main, w_f, tm, 1024)

        lb = lower_bounds[layer][None, :]
        ya = _hgrn2(pb, pf, jnp.log(lb), jnp.log1p(-lb), bsz, seq, tb)

        bias_col = jnp.concatenate([fox_fgate_bias[layer].astype(F32),
                                    jnp.zeros((SUBLANES - N_HEADS,), F32)])[:, None]
        neg_cum = _fox_neg_cum(pf, bias_col, bsz, seq)
        yb = _fox(pb, neg_cum, bsz, seq, fox_sub)

        yc = _retention(pb, tables, bsz, seq, tb)

        w_route = jnp.concatenate(
            [w_router_group[layer], w_router_expert[layer],
             jnp.zeros((d, LANES - N_GROUPS - N_EXPERTS), F32)], axis=1).astype(F32)
        wr_hi = w_route.astype(BF16)
        wr_lo = (w_route - wr_hi.astype(F32)).astype(BF16)
        x1, logits = _merge(h, ya, yb, yc, pb, w_branch[layer].astype(BF16), w_out[layer].astype(BF16),
                            ln1_g[layer][None, :], ln1_b[layer][None, :], wr_hi, wr_lo, alpha, min(256, n))

        route, rank, counts = _rank(logits, min(1024, n))
        dest, block_expert, n_used, padded_end, n_rows = _dispatch_plan(route, rank, counts, n)
        x_rows = _dispatch(x1, dest, padded_end, n_rows, min(2048, n))
        y_rows = _experts(x_rows, block_expert, n_used, w_gate, w_up, w_down, layer)
        h = _combine(x1, y_rows, dest, route, ln2_g[layer][None, :], ln2_b[layer][None, :], alpha,
                     min(512, n))
    return h.reshape(bsz, seq, d)
```

```python
import functools
import math

import numpy as np
import jax
import jax.numpy as jnp
from jax import lax
from jax.experimental import pallas as pl
from jax.experimental.pallas import tpu as pltpu

D_MODEL = 1024
HEAD_DIM = 128
MIX_WIDTH = D_MODEL // 2
N_HEADS = MIX_WIDTH // HEAD_DIM
N_GROUPS = 4
EXPERTS_PER_GROUP = 8
N_EXPERTS = N_GROUPS * EXPERTS_PER_GROUP
TOP_K = 2
D_FF_EXPERT = D_MODEL // 2
LN_EPS = 1e-5
HEAD_NORM_EPS = 1e-6
RET_ROPE_BASE = 10000.0

LANES = 128
SUBLANES = 8
HGRN_SUB = 2 * SUBLANES
HGRN_UNROLL = 16
RET_CHUNK = 256
FOX_BLOCK = 512
MOE_ROWS = 512
DMA_UNROLL = 8
VMEM_LIMIT = 48 * 1024 * 1024
LOG2E = 1.4426950408889634

COL_GA, COL_GB, COL_GC = 0, 1024, 2048
COL_AQ, COL_AI, COL_AG = 3072, 3584, 4096
COL_BQ, COL_BK, COL_BV = 4608, 5120, 5632
COL_CQ, COL_CK, COL_CV, COL_CG = 6144, 6656, 7168, 7680
N_MAIN = 8192
N_FGATE = MIX_WIDTH + LANES

F32 = jnp.float32
BF16 = jnp.bfloat16


def _cparams(sem):
    return pltpu.CompilerParams(dimension_semantics=sem, vmem_limit_bytes=VMEM_LIMIT)


def _proj_kernel(x_ref, w_ref, wf_ref, o_ref, of_ref, *, n_main):
    j = pl.program_id(1)
    xb = x_ref[...].astype(BF16)

    @pl.when(j < n_main)
    def _():
        o_ref[...] = jnp.dot(xb, w_ref[...], preferred_element_type=F32).astype(o_ref.dtype)

    @pl.when(j == n_main)
    def _():
        of_ref[...] = jnp.dot(xb, wf_ref[...], preferred_element_type=F32)


def _project(x2d, w, wf, tm, tn):
    n, d = x2d.shape
    c = w.shape[1]
    cf = wf.shape[1]
    n_main = c // tn
    last = n_main - 1
    return pl.pallas_call(
        functools.partial(_proj_kernel, n_main=n_main),
        grid=(n // tm, n_main + 1),
        in_specs=[pl.BlockSpec((tm, d), lambda i, j: (i, 0)),
                  pl.BlockSpec((d, tn), lambda i, j: (0, jnp.minimum(j, last))),
                  pl.BlockSpec((d, cf), lambda i, j: (0, 0))],
        out_specs=[pl.BlockSpec((tm, tn), lambda i, j: (i, jnp.minimum(j, last))),
                   pl.BlockSpec((tm, cf), lambda i, j: (i, 0))],
        out_shape=[jax.ShapeDtypeStruct((n, c), BF16),
                   jax.ShapeDtypeStruct((n, cf), F32)],
        compiler_params=_cparams(("parallel", "arbitrary")),
    )(x2d, w, wf)


def _log_sigmoid(z):
    return jnp.minimum(z, 0.0) - jnp.log1p(jnp.exp(-jnp.abs(z)))


def _hgrn2_kernel(q_ref, i_ref, g_ref, f_ref, loglb_ref, log1m_ref, o_ref, st_ref, rows_ref, *, tb):
    @pl.when(pl.program_id(1) == 0)
    def _():
        st_ref[...] = jnp.zeros_like(st_ref)

    row = lax.broadcasted_iota(jnp.int32, (HGRN_SUB, HEAD_DIM), 0)
    row8 = lax.broadcasted_iota(jnp.int32, (SUBLANES, HEAD_DIM), 0)
    ninf = -jnp.inf
    ones_sq = jnp.ones((HEAD_DIM, HEAD_DIM), BF16)

    def head_step(rows, h, u):
        cols = slice(h * HEAD_DIM, (h + 1) * HEAD_DIM)
        z = f_ref[rows, cols]
        a = loglb_ref[:, cols]
        b = log1m_ref[:, cols] + _log_sigmoid(z)
        log_f = jnp.maximum(a, b) + jnp.log1p(jnp.exp(-jnp.abs(a - b)))
        k = 1.0 - jnp.exp(log_f)
        q = q_ref[rows, cols].astype(F32)
        v = i_ref[rows, cols].astype(F32)
        cum = log_f * LOG2E
        for s in (1, 2, 4, 8):
            cum = cum + jnp.where(row >= s, pltpu.roll(cum, s, axis=0), 0.0)
        q_t, q_b = q[:SUBLANES], q[SUBLANES:]
        c_t, c_b = cum[:SUBLANES], cum[SUBLANES:]
        rows_ref[u, h, 0] = cum
        rows_ref[u, h, 1] = k
        rows_ref[u, h, 2] = v
        w_top, w_bot = [], []
        for s in range(HGRN_SUB):
            cs, ks = rows_ref[u, h, 0, s:s + 1, :], rows_ref[u, h, 1, s:s + 1, :]
            if s < SUBLANES:
                w_bot.append(q_b * ks * jnp.exp2(c_b - cs))
                e_t = c_t - cs
                if s > 0:
                    e_t = jnp.where(row8 >= s, e_t, ninf)
                w_top.append(q_t * ks * jnp.exp2(e_t))
            else:
                e_b = c_b - cs
                if s > SUBLANES:
                    e_b = jnp.where(row8 >= s - SUBLANES, e_b, ninf)
                w_bot.append(q_b * ks * jnp.exp2(e_b))
        scores = jnp.dot(jnp.concatenate(w_top + w_bot, axis=0).astype(BF16), ones_sq,
                         preferred_element_type=F32)
        o_t = jnp.zeros((SUBLANES, HEAD_DIM), F32)
        o_b = jnp.zeros((SUBLANES, HEAD_DIM), F32)
        for s in range(HGRN_SUB):
            vs = rows_ref[u, h, 2, s:s + 1, :]
            if s < SUBLANES:
                o_t = o_t + scores[s * SUBLANES:(s + 1) * SUBLANES] * vs
            b0 = (SUBLANES + s) * SUBLANES
            o_b = o_b + scores[b0:b0 + SUBLANES] * vs
        o = jnp.concatenate([o_t, o_b], axis=0)
        st = st_ref[h]
        qd = (q * jnp.exp2(cum)).astype(BF16)
        o = o + lax.dot_general(qd, st.astype(BF16), (((1,), (1,)), ((), ())),
                                preferred_element_type=F32)
        last = cum[HGRN_SUB - 1:HGRN_SUB, :]
        kd = (k * jnp.exp2(last - cum)).astype(BF16)
        upd = lax.dot_general(v.astype(BF16), kd, (((0,), (0,)), ((), ())),
                              preferred_element_type=F32)
        st_ref[h] = st * jnp.exp2(last) + upd
        y = o * lax.rsqrt(jnp.mean(o * o, axis=1, keepdims=True) + HEAD_NORM_EPS)
        y = y * jax.nn.sigmoid(g_ref[rows, cols].astype(F32))
        o_ref[rows, cols] = y.astype(o_ref.dtype)

    def step(j, carry):
        for u in range(HGRN_UNROLL):
            r0 = pl.multiple_of((j * HGRN_UNROLL + u) * HGRN_SUB, HGRN_SUB)
            for h in range(N_HEADS):
                head_step(pl.ds(r0, HGRN_SUB), h, u)
        return carry

    lax.fori_loop(0, tb // (HGRN_SUB * HGRN_UNROLL), step, 0)


def _hgrn2(pb, pf, log_lb, log1m_lb, bsz, seq, tb):
    n = bsz * seq
    nt = seq // tb
    cb = lambda c: c // MIX_WIDTH
    row_map = lambda c: (lambda b, t: (b * nt + t, c))
    return pl.pallas_call(
        functools.partial(_hgrn2_kernel, tb=tb),
        grid=(bsz, nt),
        in_specs=[pl.BlockSpec((tb, MIX_WIDTH), row_map(cb(COL_AQ))),
                  pl.BlockSpec((tb, MIX_WIDTH), row_map(cb(COL_AI))),
                  pl.BlockSpec((tb, MIX_WIDTH), row_map(cb(COL_AG))),
                  pl.BlockSpec((tb, MIX_WIDTH), row_map(0)),
                  pl.BlockSpec((1, MIX_WIDTH), lambda b, t: (0, 0)),
                  pl.BlockSpec((1, MIX_WIDTH), lambda b, t: (0, 0))],
        out_specs=pl.BlockSpec((tb, MIX_WIDTH), row_map(0)),
        out_shape=jax.ShapeDtypeStruct((n, MIX_WIDTH), BF16),
        scratch_shapes=[pltpu.VMEM((N_HEADS, HEAD_DIM, HEAD_DIM), F32),
                        pltpu.VMEM((HGRN_UNROLL, N_HEADS, 3, HGRN_SUB, HEAD_DIM), F32)],
        compiler_params=_cparams(("parallel", "arbitrary")),
    )(pb, pb, pb, pf, log_lb, log1m_lb)


def _retention_kernel(q_ref, k_ref, v_ref, g_ref, cos_ref, sin_ref, dmask_ref, qdec_ref, kdec_ref,
                      o_ref, st_ref, *, tb, chunk_decay):
    @pl.when(pl.program_id(1) == 0)
    def _():
        st_ref[...] = jnp.zeros_like(st_ref)

    def step(j, carry):
        r0 = pl.multiple_of(j * RET_CHUNK, RET_CHUNK)
        rows = pl.ds(r0, RET_CHUNK)
        cos = cos_ref[rows, :]
        sin = sin_ref[rows, :]
        for h in range(N_HEADS):
            cols = slice(h * HEAD_DIM, (h + 1) * HEAD_DIM)
            q = q_ref[rows, cols].astype(F32)
            k = k_ref[rows, cols].astype(F32)
            q = q * cos + pltpu.roll(q, HEAD_DIM // 2, axis=1) * sin
            k = (k * cos + pltpu.roll(k, HEAD_DIM // 2, axis=1) * sin) * (HEAD_DIM ** -0.5)
            v = v_ref[rows, cols]
            inner = lax.dot_general(q.astype(BF16), k.astype(BF16), (((1,), (1,)), ((), ())),
                                    preferred_element_type=F32) * dmask_ref[h]
            st = st_ref[h]
            o = (jnp.dot(inner.astype(BF16), v, preferred_element_type=F32)
                 + jnp.dot((q * qdec_ref[h]).astype(BF16), st.astype(BF16),
                           preferred_element_type=F32))
            upd = lax.dot_general((k * kdec_ref[h]).astype(BF16), v, (((0,), (0,)), ((), ())),
                                  preferred_element_type=F32)
            st_ref[h] = st * chunk_decay[h] + upd
            c = o - jnp.mean(o, axis=1, keepdims=True)
            y = c * lax.rsqrt(jnp.mean(c * c, axis=1, keepdims=True) + HEAD_NORM_EPS)
            g = g_ref[rows, cols].astype(F32)
            o_ref[rows, cols] = (y * (g * jax.nn.sigmoid(g))).astype(o_ref.dtype)
        return carry

    lax.fori_loop(0, tb // RET_CHUNK, step, 0)


def _retention_tables(seq):
    half = HEAD_DIM // 2
    inv = 1.0 / (RET_ROPE_BASE ** jnp.linspace(0.0, 1.0, half, dtype=F32))
    ang = jnp.arange(seq, dtype=F32)[:, None] * inv[None, :]
    cos = jnp.cos(ang)
    sin = jnp.sin(ang)
    cos_t = jnp.concatenate([cos, cos], axis=-1)
    sin_t = jnp.concatenate([-sin, sin], axis=-1)
    log_gamma = jnp.log(1.0 - jnp.power(2.0, -5.0 - jnp.arange(N_HEADS, dtype=F32)))
    idx = jnp.arange(RET_CHUNK, dtype=F32)
    rel = idx[:, None] - idx[None, :]
    dmask = jnp.where(rel >= 0, jnp.exp(log_gamma[:, None, None] * jnp.maximum(rel, 0.0)), 0.0)
    ones = jnp.ones((1, 1, HEAD_DIM), F32)
    qdec = jnp.exp(log_gamma[:, None] * (idx + 1.0))[..., None] * ones
    kdec = jnp.exp(log_gamma[:, None] * (RET_CHUNK - 1.0 - idx))[..., None] * ones
    return cos_t, sin_t, dmask, qdec, kdec


def _retention(pb, tables, bsz, seq, tb):
    n = bsz * seq
    nt = seq // tb
    cos_t, sin_t, dmask, qdec, kdec = tables
    chunk_decay = tuple(float((1.0 - 2.0 ** (-5.0 - h)) ** RET_CHUNK) for h in range(N_HEADS))
    cb = lambda c: c // MIX_WIDTH
    row_map = lambda c: (lambda b, t: (b * nt + t, c))
    const3 = lambda b, t: (0, 0, 0)
    return pl.pallas_call(
        functools.partial(_retention_kernel, tb=tb, chunk_decay=chunk_decay),
        grid=(bsz, nt),
        in_specs=[pl.BlockSpec((tb, MIX_WIDTH), row_map(cb(COL_CQ))),
                  pl.BlockSpec((tb, MIX_WIDTH), row_map(cb(COL_CK))),
                  pl.BlockSpec((tb, MIX_WIDTH), row_map(cb(COL_CV))),
                  pl.BlockSpec((tb, MIX_WIDTH), row_map(cb(COL_CG))),
                  pl.BlockSpec((tb, HEAD_DIM), lambda b, t: (t, 0)),
                  pl.BlockSpec((tb, HEAD_DIM), lambda b, t: (t, 0)),
                  pl.BlockSpec((N_HEADS, RET_CHUNK, RET_CHUNK), const3),
                  pl.BlockSpec((N_HEADS, RET_CHUNK, HEAD_DIM), const3),
                  pl.BlockSpec((N_HEADS, RET_CHUNK, HEAD_DIM), const3)],
        out_specs=pl.BlockSpec((tb, MIX_WIDTH), row_map(0)),
        out_shape=jax.ShapeDtypeStruct((n, MIX_WIDTH), BF16),
        scratch_shapes=[pltpu.VMEM((N_HEADS, HEAD_DIM, HEAD_DIM), F32)],
        compiler_params=_cparams(("parallel", "arbitrary")),
    )(pb, pb, pb, pb, cos_t, sin_t, dmask, qdec, kdec)


def _fox_cum_kernel(z_ref, bias_ref, o_ref):
    z = z_ref[...]
    hi = z.astype(BF16)
    rem = z - hi.astype(F32)
    mid = rem.astype(BF16)
    lo = (rem - mid.astype(F32)).astype(BF16)
    sel = jnp.where(lax.broadcasted_iota(jnp.int32, (SUBLANES, LANES), 0)
                    == lax.broadcasted_iota(jnp.int32, (SUBLANES, LANES), 1), 1.0, 0.0).astype(BF16)
    nt = (((1,), (1,)), ((), ()))
    zt = (lax.dot_general(sel, hi, nt, preferred_element_type=F32)
          + lax.dot_general(sel, mid, nt, preferred_element_type=F32)
          + lax.dot_general(sel, lo, nt, preferred_element_type=F32))
    x = _log_sigmoid(zt + bias_ref[...])
    lane = lax.broadcasted_iota(jnp.int32, x.shape, 1)
    s = 1
    while s < x.shape[1]:
        x = x + jnp.where(lane >= s, pltpu.roll(x, s, axis=1), 0.0)
        s *= 2
    o_ref[...] = x * (-LOG2E)


def _fox_neg_cum(pf, bias_col, bsz, seq):
    return pl.pallas_call(
        _fox_cum_kernel,
        grid=(bsz,),
        in_specs=[pl.BlockSpec((seq, LANES), lambda b: (b, MIX_WIDTH // LANES)),
                  pl.BlockSpec((SUBLANES, 1), lambda b: (0, 0))],
        out_specs=pl.BlockSpec((None, SUBLANES, seq), lambda b: (b, 0, 0)),
        out_shape=jax.ShapeDtypeStruct((bsz, SUBLANES, seq), F32),
        compiler_params=_cparams(("parallel",)),
    )(pf, bias_col)


def _fox_kernel(q_ref, k_ref, v_ref, nck_ref, o_ref, m_ref, acc_ref, *, nsub, scale):
    i = pl.program_id(2)
    tk = FOX_BLOCK
    m_ref[...] = jnp.full_like(m_ref, -jnp.inf)
    acc_ref[...] = jnp.zeros_like(acc_ref)
    qs = [(q_ref[a * tk:(a + 1) * tk, :].astype(F32) * (scale * LOG2E)).astype(BF16)
          for a in range(nsub)]
    ones = jnp.ones((tk, HEAD_DIM), BF16)
    causal = (lax.broadcasted_iota(jnp.int32, (tk, tk), 1)
              <= lax.broadcasted_iota(jnp.int32, (tk, tk), 0))

    def attend(a, kblk, masked):
        rows = pl.ds(pl.multiple_of(kblk * tk, tk), tk)
        s = lax.dot_general(qs[a], k_ref[rows, :], (((1,), (1,)), ((), ())),
                            preferred_element_type=F32) + nck_ref[kblk]
        if masked:
            s = jnp.where(causal, s, -jnp.inf)
        m_prev = m_ref[a]
        m_new = jnp.maximum(m_prev, jnp.max(s, axis=1, keepdims=True))
        p = jnp.exp2(s - jnp.concatenate([m_new] * (tk // LANES), axis=1)).astype(BF16)
        v_aug = jnp.concatenate([v_ref[rows, :], ones], axis=1)
        alpha = jnp.exp2(m_prev - m_new)
        acc_ref[a] = jnp.concatenate([alpha, alpha], axis=1) * acc_ref[a] + jnp.dot(
            p, v_aug, preferred_element_type=F32)
        m_ref[a] = m_new

    def below_diagonal(j, c):
        for a in range(nsub):
            attend(a, j, False)
        return c

    lax.fori_loop(0, i * nsub, below_diagonal, 0)
    for d in range(nsub):
        for a in range(d, nsub):
            attend(a, i * nsub + d, a == d)
    for a in range(nsub):
        acc = acc_ref[a]
        o_ref[a * tk:(a + 1) * tk, :] = (acc[:, :HEAD_DIM] / acc[:, HEAD_DIM:]).astype(o_ref.dtype)


def _fox(pb, neg_cum, bsz, seq, nsub):
    n = bsz * seq
    tq = nsub * FOX_BLOCK
    nq = seq // tq
    nkb = seq // FOX_BLOCK
    cq, ck, cv = COL_BQ // HEAD_DIM, COL_BK // HEAD_DIM, COL_BV // HEAD_DIM
    return pl.pallas_call(
        functools.partial(_fox_kernel, nsub=nsub, scale=HEAD_DIM ** -0.5),
        grid=(bsz, N_HEADS, nq),
        in_specs=[pl.BlockSpec((tq, HEAD_DIM), lambda b, h, i: (b * nq + i, cq + h)),
                  pl.BlockSpec((seq, HEAD_DIM), lambda b, h, i: (b, ck + h)),
                  pl.BlockSpec((seq, HEAD_DIM), lambda b, h, i: (b, cv + h)),
                  pl.BlockSpec((None, None, nkb, 1, FOX_BLOCK), lambda b, h, i: (b, h, 0, 0, 0))],
        out_specs=pl.BlockSpec((tq, HEAD_DIM), lambda b, h, i: (b * nq + i, h)),
        out_shape=jax.ShapeDtypeStruct((n, MIX_WIDTH), BF16),
        scratch_shapes=[pltpu.VMEM((nsub, FOX_BLOCK, LANES), F32),
                        pltpu.VMEM((nsub, FOX_BLOCK, 2 * HEAD_DIM), F32)],
        compiler_params=_cparams(("parallel", "parallel", "arbitrary")),
    )(pb, pb, pb, neg_cum.reshape(bsz, SUBLANES, nkb, 1, FOX_BLOCK))


def _layer_norm(x, g, b):
    mu = jnp.mean(x, axis=1, keepdims=True)
    c = x - mu
    var = jnp.mean(c * c, axis=1, keepdims=True)
    return c * lax.rsqrt(var + LN_EPS) * g + b


def _route(logits):
    lane = lax.broadcasted_iota(jnp.int32, logits.shape, 1)
    lane_f = lane.astype(F32)
    ninf = -jnp.inf
    big = float(LANES)
    gl = jnp.where(lane < N_GROUPS, logits, ninf)
    gmax = jnp.max(gl, axis=1, keepdims=True)
    gidx = jnp.min(jnp.where(gl == gmax, lane_f, big), axis=1, keepdims=True)
    gprob = 1.0 / jnp.sum(jnp.exp(gl - gmax), axis=1, keepdims=True)
    e_group = ((lane - N_GROUPS) // EXPERTS_PER_GROUP).astype(F32)
    in_grp = (lane >= N_GROUPS) & (lane < N_GROUPS + N_EXPERTS) & (e_group == gidx)
    el = jnp.where(in_grp, logits, ninf)
    t1 = jnp.max(el, axis=1, keepdims=True)
    i1 = jnp.min(jnp.where(el == t1, lane_f, big), axis=1, keepdims=True)
    el2 = jnp.where(lane_f == i1, ninf, el)
    t2 = jnp.max(el2, axis=1, keepdims=True)
    i2 = jnp.min(jnp.where(el2 == t2, lane_f, big), axis=1, keepdims=True)
    d = jnp.exp(t2 - t1)
    g1 = gprob / (1.0 + d)
    g2 = gprob * d / (1.0 + d)
    out = jnp.where(lane == 0, i1 - N_GROUPS, 0.0)
    out = jnp.where(lane == 1, i2 - N_GROUPS, out)
    out = jnp.where(lane == 2, g1, out)
    out = jnp.where(lane == 3, g2, out)
    return out


def _merge_kernel(x_ref, ya_ref, yb_ref, yc_ref, ga_ref, gb_ref, gc_ref, wb_ref, wo_ref,
                  lng_ref, lnb_ref, wrh_ref, wrl_ref, x1_ref, logit_ref, *, alpha):
    def branch(y_ref, g_ref, idx):
        return jax.nn.sigmoid(g_ref[...].astype(F32)) * jnp.dot(
            y_ref[...], wb_ref[idx], preferred_element_type=F32)

    merged = branch(ya_ref, ga_ref, 0) + branch(yb_ref, gb_ref, 1) + branch(yc_ref, gc_ref, 2)
    mix = jnp.dot(merged.astype(BF16), wo_ref[...], preferred_element_type=F32)
    x1 = _layer_norm(alpha * x_ref[...] + mix, lng_ref[...], lnb_ref[...])
    x1_ref[...] = x1
    x_hi = x1.astype(BF16)
    x_lo = (x1 - x_hi.astype(F32)).astype(BF16)
    logits = (jnp.dot(x_hi, wrh_ref[...], preferred_element_type=F32)
              + jnp.dot(x_lo, wrh_ref[...], preferred_element_type=F32)
              + jnp.dot(x_hi, wrl_ref[...], preferred_element_type=F32))
    logit_ref[...] = logits


def _merge(x2d, ya, yb, yc, pb, wb, wo, lng, lnb, wr_hi, wr_lo, alpha, tm):
    n = x2d.shape[0]
    row = lambda c: (lambda i: (i, c))
    const2 = lambda i: (0, 0)
    return pl.pallas_call(
        functools.partial(_merge_kernel, alpha=alpha),
        grid=(n // tm,),
        in_specs=[pl.BlockSpec((tm, D_MODEL), row(0)),
                  pl.BlockSpec((tm, MIX_WIDTH), row(0)),
                  pl.BlockSpec((tm, MIX_WIDTH), row(0)),
                  pl.BlockSpec((tm, MIX_WIDTH), row(0)),
                  pl.BlockSpec((tm, D_MODEL), row(COL_GA // D_MODEL)),
                  pl.BlockSpec((tm, D_MODEL), row(COL_GB // D_MODEL)),
                  pl.BlockSpec((tm, D_MODEL), row(COL_GC // D_MODEL)),
                  pl.BlockSpec((3, MIX_WIDTH, D_MODEL), lambda i: (0, 0, 0)),
                  pl.BlockSpec((D_MODEL, D_MODEL), const2),
                  pl.BlockSpec((1, D_MODEL), const2),
                  pl.BlockSpec((1, D_MODEL), const2),
                  pl.BlockSpec((D_MODEL, LANES), const2),
                  pl.BlockSpec((D_MODEL, LANES), const2)],
        out_specs=[pl.BlockSpec((tm, D_MODEL), row(0)),
                   pl.BlockSpec((tm, LANES), row(0))],
        out_shape=[jax.ShapeDtypeStruct((n, D_MODEL), F32),
                   jax.ShapeDtypeStruct((n, LANES), F32)],
        compiler_params=_cparams(("parallel",)),
    )(x2d, ya, yb, yc, pb, pb, pb, wb, wo, lng, lnb, wr_hi, wr_lo)


def _rank_kernel(logit_ref, route_ref, rank_ref, cnt_ref, carry_ref, *, tr):
    @pl.when(pl.program_id(0) == 0)
    def _():
        carry_ref[...] = jnp.zeros_like(carry_ref)

    r = _route(logit_ref[...])
    route_ref[...] = r
    lane = lax.broadcasted_iota(jnp.int32, r.shape, 1)
    lane_f = lane.astype(F32)
    oh1 = lane_f == r[:, 0:1]
    oh2 = lane_f == r[:, 1:2]
    oh = jnp.where(oh1, 1.0, jnp.where(oh2, 1.0, 0.0))
    earlier = jnp.where(lax.broadcasted_iota(jnp.int32, (tr, tr), 1)
                        < lax.broadcasted_iota(jnp.int32, (tr, tr), 0), 1.0, 0.0).astype(BF16)
    before = jnp.dot(earlier, oh.astype(BF16), preferred_element_type=F32) + carry_ref[...]
    rank1 = jnp.sum(jnp.where(oh1, before, 0.0), axis=1, keepdims=True)
    rank2 = jnp.sum(jnp.where(oh2, before, 0.0), axis=1, keepdims=True)
    rank_ref[...] = jnp.where(lane == 0, rank1, jnp.where(lane == 1, rank2, 0.0))
    total = carry_ref[...] + jnp.sum(oh, axis=0, keepdims=True)
    carry_ref[...] = total
    cnt_ref[...] = total


def _rank(logits, tr):
    n = logits.shape[0]
    return pl.pallas_call(
        functools.partial(_rank_kernel, tr=tr),
        grid=(n // tr,),
        in_specs=[pl.BlockSpec((tr, LANES), lambda i: (i, 0))],
        out_specs=[pl.BlockSpec((tr, LANES), lambda i: (i, 0)),
                   pl.BlockSpec((tr, LANES), lambda i: (i, 0)),
                   pl.BlockSpec((1, LANES), lambda i: (0, 0))],
        out_shape=[jax.ShapeDtypeStruct((n, LANES), F32),
                   jax.ShapeDtypeStruct((n, LANES), F32),
                   jax.ShapeDtypeStruct((1, LANES), F32)],
        scratch_shapes=[pltpu.VMEM((1, LANES), F32)],
        compiler_params=_cparams(("arbitrary",)),
    )(logits)


def _dispatch_plan(route, rank, counts, n_tok):
    n_rows = n_tok * TOP_K + N_EXPERTS * MOE_ROWS
    n_blocks = n_rows // MOE_ROWS
    cnt = counts[0, :N_EXPERTS].astype(jnp.int32)
    padded = (cnt + MOE_ROWS - 1) // MOE_ROWS * MOE_ROWS
    padded_end = jnp.cumsum(padded)
    padded_start = padded_end - padded
    expert = route[:, :TOP_K].astype(jnp.int32)
    start = jnp.sum(jnp.where(expert[:, :, None] == jnp.arange(N_EXPERTS)[None, None, :],
                              padded_start[None, None, :], 0), axis=-1)
    dest = start + rank[:, :TOP_K].astype(jnp.int32)
    block_start = jnp.arange(n_blocks, dtype=jnp.int32) * MOE_ROWS
    block_expert = jnp.minimum(jnp.sum(block_start[:, None] >= padded_end[None, :], axis=1),
                               N_EXPERTS - 1).astype(jnp.int32)
    n_used = (padded_end[-1:] // MOE_ROWS).astype(jnp.int32)
    return dest, block_expert, n_used, padded_end.astype(jnp.int32), n_rows


def _dispatch_kernel(pend_ref, d0_ref, d1_ref, x_ref, xr_hbm, zbuf, sem_z, sem, *, tmb, n_blocks):
    i = pl.program_id(0)

    def zero_copy(e):
        end = pend_ref[e]
        return pltpu.make_async_copy(
            zbuf, xr_hbm.at[pl.ds(pl.multiple_of(end - MOE_ROWS, MOE_ROWS), MOE_ROWS), :], sem_z)

    def nonempty(e):
        return pend_ref[e] > (pend_ref[e - 1] if e > 0 else 0)

    @pl.when(i == 0)
    def _():
        zbuf[...] = jnp.zeros_like(zbuf)
        for e in range(N_EXPERTS):
            @pl.when(nonempty(e))
            def _():
                zero_copy(e).start()
        for e in range(N_EXPERTS):
            @pl.when(nonempty(e))
            def _():
                zero_copy(e).wait()

        def tail_copy(b):
            return pltpu.make_async_copy(
                zbuf, xr_hbm.at[pl.ds(pl.multiple_of(b * MOE_ROWS, MOE_ROWS), MOE_ROWS), :], sem_z)

        def tail_start(b, c):
            tail_copy(b).start()
            return c

        def tail_wait(b, c):
            tail_copy(b).wait()
            return c

        first_unused = pend_ref[N_EXPERTS - 1] // MOE_ROWS
        lax.fori_loop(first_unused, n_blocks, tail_start, 0)
        lax.fori_loop(first_unused, n_blocks, tail_wait, 0)

    def row_copy(r, dst):
        return pltpu.make_async_copy(x_ref.at[pl.ds(r, 1), :], xr_hbm.at[pl.ds(dst, 1), :], sem)

    def issue(g, c):
        for u in range(DMA_UNROLL):
            r = g * DMA_UNROLL + u
            row_copy(r, d0_ref[0, 0, r]).start()
            row_copy(r, d1_ref[0, 0, r]).start()
        return c

    lax.fori_loop(0, tmb // DMA_UNROLL, issue, 0)

    for _ in range(TOP_K):
        pltpu.make_async_copy(x_ref, xr_hbm.at[pl.ds(0, tmb), :], sem).wait()


def _dispatch(x1, dest, padded_end, n_rows, tmb):
    n = x1.shape[0]
    nb = n // tmb
    idx_spec = pl.BlockSpec((1, 1, tmb), lambda i, pe: (i, 0, 0), memory_space=pltpu.SMEM)
    grid_spec = pltpu.PrefetchScalarGridSpec(
        num_scalar_prefetch=1,
        grid=(nb,),
        in_specs=[idx_spec, idx_spec, pl.BlockSpec((tmb, D_MODEL), lambda i, pe: (i, 0))],
        out_specs=pl.BlockSpec(memory_space=pl.ANY),
        scratch_shapes=[pltpu.VMEM((MOE_ROWS, D_MODEL), F32),
                        pltpu.SemaphoreType.DMA, pltpu.SemaphoreType.DMA],
    )
    return pl.pallas_call(
        functools.partial(_dispatch_kernel, tmb=tmb, n_blocks=n_rows // MOE_ROWS),
        grid_spec=grid_spec,
        out_shape=jax.ShapeDtypeStruct((n_rows, D_MODEL), F32),
        compiler_params=_cparams(("arbitrary",)),
    )(padded_end, dest[:, 0].reshape(nb, 1, tmb), dest[:, 1].reshape(nb, 1, tmb), x1)


def _expert_kernel(bexp_ref, nused_ref, x_ref, wg_ref, wu_ref, wd_ref, y_ref, wg_s, wu_s, wd_s):
    i = pl.program_id(0)
    used = i < nused_ref[0]
    new_expert = (i == 0) | (bexp_ref[i] != bexp_ref[jnp.maximum(i - 1, 0)])

    @pl.when(used & new_expert)
    def _():
        wg_s[...] = wg_ref[0].astype(BF16)
        wu_s[...] = wu_ref[0].astype(BF16)
        wd_s[...] = wd_ref[0].astype(BF16)

    @pl.when(used)
    def _():
        xb = x_ref[...].astype(BF16)
        hg = jnp.dot(xb, wg_s[...], preferred_element_type=F32)
        hu = jnp.dot(xb, wu_s[...], preferred_element_type=F32)
        hid = (hg * jax.nn.sigmoid(hg)) * hu
        y_ref[...] = jnp.dot(hid.astype(BF16), wd_s[...], preferred_element_type=F32)

    @pl.when(jnp.logical_not(used))
    def _():
        y_ref[...] = jnp.zeros_like(y_ref)


def _experts(x_rows, block_expert, n_used, wg, wu, wd, layer):
    n_rows = x_rows.shape[0]
    n_blocks = n_rows // MOE_ROWS
    w_map = lambda i, be, nu: (layer, be[i], 0, 0)
    grid_spec = pltpu.PrefetchScalarGridSpec(
        num_scalar_prefetch=2,
        grid=(n_blocks,),
        in_specs=[pl.BlockSpec((MOE_ROWS, D_MODEL), lambda i, be, nu: (jnp.minimum(i, nu[0] - 1), 0)),
                  pl.BlockSpec((None, 1, D_MODEL, D_FF_EXPERT), w_map),
                  pl.BlockSpec((None, 1, D_MODEL, D_FF_EXPERT), w_map),
                  pl.BlockSpec((None, 1, D_FF_EXPERT, D_MODEL), w_map)],
        out_specs=pl.BlockSpec((MOE_ROWS, D_MODEL), lambda i, be, nu: (i, 0)),
        scratch_shapes=[pltpu.VMEM((D_MODEL, D_FF_EXPERT), BF16),
                        pltpu.VMEM((D_MODEL, D_FF_EXPERT), BF16),
                        pltpu.VMEM((D_FF_EXPERT, D_MODEL), BF16)],
    )
    return pl.pallas_call(
        _expert_kernel,
        grid_spec=grid_spec,
        out_shape=jax.ShapeDtypeStruct((n_rows, D_MODEL), F32),
        compiler_params=_cparams(("arbitrary",)),
    )(block_expert, n_used, x_rows, wg, wu, wd)


def _combine_kernel(d0_ref, d1_ref, d0n_ref, d1n_ref, x_ref, route_ref, lng_ref, lnb_ref, y_hbm,
                    o_ref, ybuf, sems, *, alpha, tc, nb):
    i = pl.program_id(0)

    def row_copy(src, slot, choice, r):
        return pltpu.make_async_copy(y_hbm.at[pl.ds(src, 1), :],
                                     ybuf.at[slot, choice, pl.ds(r, 1), :], sems.at[slot])

    def fetch(da_ref, db_ref, slot):
        def issue(g, c):
            for u in range(DMA_UNROLL):
                r = g * DMA_UNROLL + u
                row_copy(da_ref[0, 0, r], slot, 0, r).start()
                row_copy(db_ref[0, 0, r], slot, 1, r).start()
            return c

        lax.fori_loop(0, tc // DMA_UNROLL, issue, 0)

    @pl.when(i == 0)
    def _():
        fetch(d0_ref, d1_ref, 0)

    for slot in range(2):
        @pl.when((i + 1 < nb) & ((i + 1) % 2 == slot))
        def _():
            fetch(d0n_ref, d1n_ref, slot)

    slot = i % 2

    for choice in range(TOP_K):
        pltpu.make_async_copy(y_hbm.at[pl.ds(0, tc), :], ybuf.at[slot, choice], sems.at[slot]).wait()

    r = route_ref[...]
    ffn = r[:, 2:3] * ybuf[slot, 0] + r[:, 3:4] * ybuf[slot, 1]
    o_ref[...] = _layer_norm(alpha * x_ref[...] + ffn, lng_ref[...], lnb_ref[...])


def _combine(x1, y_rows, dest, route, lng, lnb, alpha, tc):
    n = x1.shape[0]
    nb = n // tc
    const2 = lambda i: (0, 0)
    cur = pl.BlockSpec((1, 1, tc), lambda i: (i, 0, 0), memory_space=pltpu.SMEM)
    nxt = pl.BlockSpec((1, 1, tc), lambda i: (jnp.minimum(i + 1, nb - 1), 0, 0), memory_space=pltpu.SMEM)
    d0 = dest[:, 0].reshape(nb, 1, tc)
    d1 = dest[:, 1].reshape(nb, 1, tc)
    return pl.pallas_call(
        functools.partial(_combine_kernel, alpha=alpha, tc=tc, nb=nb),
        grid=(nb,),
        in_specs=[cur, cur, nxt, nxt,
                  pl.BlockSpec((tc, D_MODEL), lambda i: (i, 0)),
                  pl.BlockSpec((tc, LANES), lambda i: (i, 0)),
                  pl.BlockSpec((1, D_MODEL), const2),
                  pl.BlockSpec((1, D_MODEL), const2),
                  pl.BlockSpec(memory_space=pl.ANY)],
        out_specs=pl.BlockSpec((tc, D_MODEL), lambda i: (i, 0)),
        out_shape=jax.ShapeDtypeStruct((n, D_MODEL), F32),
        scratch_shapes=[pltpu.VMEM((2, TOP_K, tc, D_MODEL), F32),
                        pltpu.SemaphoreType.DMA((2,))],
        compiler_params=_cparams(("arbitrary",)),
    )(d0, d1, d0, d1, x1, route, lng, lnb, y_rows)


def _take_cols_kernel(src_ref, a_ref, b_ref, o_ref, *, shift):
    del src_ref
    if shift == 0:
        o_ref[...] = a_ref[...].astype(o_ref.dtype)
    else:
        lane = lax.broadcasted_iota(jnp.int32, a_ref.shape, 1)
        o_ref[...] = jnp.where(lane < LANES - shift,
                               pltpu.roll(a_ref[...], LANES - shift, axis=1),
                               pltpu.roll(b_ref[...], LANES - shift, axis=1)).astype(o_ref.dtype)


def _take_cols(w_all, layer, src_tiles, shift):
    d = w_all.shape[1]
    last = (w_all.shape[2] - 1) // LANES
    src = jnp.asarray(np.asarray(src_tiles, np.int32))
    grid_spec = pltpu.PrefetchScalarGridSpec(
        num_scalar_prefetch=1,
        grid=(len(src_tiles),),
        in_specs=[pl.BlockSpec((None, d, LANES), lambda j, s: (layer, 0, s[j])),
                  pl.BlockSpec((None, d, LANES), lambda j, s: (layer, 0, jnp.minimum(s[j] + 1, last)))],
        out_specs=pl.BlockSpec((d, LANES), lambda j, s: (0, j)),
    )
    return pl.pallas_call(
        functools.partial(_take_cols_kernel, shift=shift),
        grid_spec=grid_spec,
        out_shape=jax.ShapeDtypeStruct((d, len(src_tiles) * LANES), BF16),
        compiler_params=_cparams(("arbitrary",)),
    )(src, w_all, w_all)


def _permute_w_in(w_all, layer):
    per = MIX_WIDTH // LANES
    tiles = lambda t0, nt: list(range(t0, t0 + nt))
    pre = 7 * per
    aligned = tiles(0, per) + tiles(2 * per, 2 * per) + tiles(4 * per, 3 * per)
    gates = tiles(pre + 4 * per, 3 * D_MODEL // LANES)
    c_part = tiles(pre, 4 * per)
    main = jnp.concatenate([_take_cols(w_all, layer, gates, N_HEADS),
                            _take_cols(w_all, layer, aligned, 0),
                            _take_cols(w_all, layer, c_part, N_HEADS)], axis=1)
    fpart = _take_cols(w_all, layer, tiles(per, per) + [pre], 0)
    return main, fpart


def kernel(x, w_in, w_branch, w_out, fox_fgate_bias, hgrn_lb_logits, ln1_g, ln1_b,
           w_router_group, w_router_expert, w_up, w_gate, w_down, ln2_g, ln2_b):
    bsz, seq, d = x.shape
    depth = w_in.shape[0]
    n = bsz * seq
    alpha = float((2 * depth) ** 0.25)
    tm = min(1024, n)
    tb = min(512, seq)
    fox_sub = max(s for s in (1, 2, 4, 8) if seq % (s * FOX_BLOCK) == 0)

    lb_cum = jnp.cumsum(jax.nn.softmax(hgrn_lb_logits.astype(F32), axis=0), axis=0)
    lower_bounds = lb_cum - lb_cum[0]
    tables = _retention_tables(seq)

    h = x.reshape(n, d)
    for layer in range(depth):
        w_main, w_f = _permute_w_in(w_in, layer)
        pb, pf = _project(h, w_main, w_f, tm, 1024)

        lb = lower_bounds[layer][None, :]
        ya = _hgrn2(pb, pf, jnp.log(lb), jnp.log1p(-lb), bsz, seq, tb)

        bias_col = jnp.concatenate([fox_fgate_bias[layer].astype(F32),
                                    jnp.zeros((SUBLANES - N_HEADS,), F32)])[:, None]
        neg_cum = _fox_neg_cum(pf, bias_col, bsz, seq)
        yb = _fox(pb, neg_cum, bsz, seq, fox_sub)

        yc = _retention(pb, tables, bsz, seq, tb)

        w_route = jnp.concatenate(
            [w_router_group[layer], w_router_expert[layer],
             jnp.zeros((d, LANES - N_GROUPS - N_EXPERTS), F32)], axis=1).astype(F32)
        wr_hi = w_route.astype(BF16)
        wr_lo = (w_route - wr_hi.astype(F32)).astype(BF16)
        x1, logits = _merge(h, ya, yb, yc, pb, w_branch[layer].astype(BF16), w_out[layer].astype(BF16),
                            ln1_g[layer][None, :], ln1_b[layer][None, :], wr_hi, wr_lo, alpha, min(256, n))

        route, rank, counts = _rank(logits, min(1024, n))
        dest, block_expert, n_used, padded_end, n_rows = _dispatch_plan(route, rank, counts, n)
        x_rows = _dispatch(x1, dest, padded_end, n_rows, min(2048, n))
        y_rows = _experts(x_rows, block_expert, n_used, w_gate, w_up, w_down, layer)
        h = _combine(x1, y_rows, dest, route, ln2_g[layer][None, :], ln2_b[layer][None, :], alpha,
                     min(512, n))
    return h.reshape(bsz, seq, d)
```
